```python
import math
import jax
import jax.numpy as jnp
from jax import lax
import numpy as np

D_MODEL = 1024
BATCH = 8
SEQ = 2048
DEPTH = 4
DEC_BATCH = 128
DEC_SEQ = 8
PAST_LEN = 2048
PAGE_SIZE = 128

HEAD_DIM = 64
N_HEADS = D_MODEL // HEAD_DIM
H_M = N_HEADS // 2
H_A = N_HEADS - H_M
W_M = H_M * HEAD_DIM
W_A = H_A * HEAD_DIM
CONV_W = 4
MLSTM_CHUNK = 128
DIL_BRANCHES = ((128, 1), (512, 4), (2048, 16))
WINDOW_MAX = 2048
ATT_BLOCK = 128
N_BUCKETS = 32
MAX_DIST = 2048
D_FF = ((8 * D_MODEL + 3 * 256 - 1) // (3 * 256)) * 256
D_PLE = 256
PROJ_SPLITS = (W_M, W_M, W_M, W_M, H_M, H_M, W_A, W_A, W_A)
D_IN = sum(PROJ_SPLITS)
EPS = 1e-6

kernel_name = 'mlstm_dilated_attn_hybrid_step'

F32 = jnp.float32


def _rmsnorm(x, g):
    xf = x.astype(F32)
    y = xf * lax.rsqrt(jnp.mean(xf * xf, axis=-1, keepdims=True) + EPS)
    return (y * g.astype(F32)).astype(x.dtype)


def _rel_bucket(dist):
    max_exact = N_BUCKETS // 2
    df = jnp.maximum(dist, 1).astype(F32)
    large = max_exact + (jnp.log(df / max_exact) / math.log(MAX_DIST / max_exact)
                         * (N_BUCKETS - max_exact)).astype(jnp.int32)
    large = jnp.minimum(large, N_BUCKETS - 1)
    return jnp.where(dist < max_exact, dist, large)


def _split_proj(z):
    parts = []
    off = 0
    for w in PROJ_SPLITS:
        parts.append(z[..., off:off + w])
        off += w
    return parts


def _short_conv(u, buf, w, b):
    L = u.shape[1]
    padded = jnp.concatenate([buf.astype(u.dtype), u], axis=1)
    y = b
    for j in range(CONV_W):
        y = y + padded[:, j:j + L] * w[j]
    return jax.nn.silu(y), padded[:, -(CONV_W - 1):]


def _mlstm_chunk(q, k, v, ig, lf, C, n, m):
    L = q.shape[2]
    b = jnp.cumsum(lf, axis=-1)
    a = b + m[..., None]
    D = b[..., :, None] - b[..., None, :] + ig[..., None, :]
    causal = jnp.tril(jnp.ones((L, L), dtype=bool))
    D = jnp.where(causal, D, -jnp.inf)
    m_t = jnp.maximum(a, jnp.max(D, axis=-1))
    inter = jnp.exp(a - m_t)
    S = jnp.einsum('bhte,bhse->bhts', q, k) * jnp.exp(D - m_t[..., None])
    num = inter[..., None] * jnp.einsum('bhte,bhfe->bhtf', q, C) + jnp.einsum('bhts,bhsf->bhtf', S, v)
    den = inter * jnp.einsum('bhte,bhe->bht', q, n) + jnp.sum(S, axis=-1)
    h = num / jnp.maximum(jnp.abs(den), jnp.exp(-m_t))[..., None]
    m_last = m_t[..., -1]
    w_end = jnp.exp(b[..., -1:] - b + ig - m_last[..., None])
    decay = jnp.exp(a[..., -1] - m_last)
    C_new = decay[..., None, None] * C + jnp.einsum('bhs,bhsf,bhse->bhfe', w_end, v, k)
    n_new = decay[..., None] * n + jnp.einsum('bhs,bhse->bhe', w_end, k)
    return h, C_new, n_new, m_last


def _mlstm_prompt(q, k, v, ig, lf):
    B, S, H, E = q.shape
    nc = S // MLSTM_CHUNK

    def to_chunks(t):
        t = jnp.moveaxis(t, 2, 1)
        t = t.reshape((B, H, nc, MLSTM_CHUNK) + t.shape[3:])
        return jnp.moveaxis(t, 2, 0)

    xs = (to_chunks(q), to_chunks(k), to_chunks(v), to_chunks(ig), to_chunks(lf))
    init = (jnp.zeros((B, H, E, E), F32), jnp.zeros((B, H, E), F32), jnp.zeros((B, H), F32))

    def step(carry, inp):
        h, C, n, m = _mlstm_chunk(inp[0], inp[1], inp[2], inp[3], inp[4], carry[0], carry[1], carry[2])
        return (C, n, m), h

    (C, n, m), hs = lax.scan(step, init, xs)
    h = jnp.moveaxis(hs, 0, 2).reshape(B, H, S, E)
    return jnp.moveaxis(h, 1, 2), C, n, m


def _mlstm_sample(q, k, v, ig, lf, C, n, m):
    h, C, n, m = _mlstm_chunk(jnp.moveaxis(q, 1, 2), jnp.moveaxis(k, 1, 2), jnp.moveaxis(v, 1, 2),
                              jnp.moveaxis(ig, 1, 2), jnp.moveaxis(lf, 1, 2), C, n, m)
    return jnp.moveaxis(h, 1, 2), C, n, m


def _dilated_branch_prompt(q, k, v, rel_bias, window, dil):
    B, S, H, E = q.shape
    L = S // dil
    wc = window // dil
    nb = -(-L // ATT_BLOCK)
    lp = nb * ATT_BLOCK

    def classes(t):
        t = t.reshape(B, L, dil, H, E).transpose(0, 2, 1, 3, 4)
        return jnp.pad(t, ((0, 0), (0, 0), (0, lp - L), (0, 0), (0, 0)))

    qb = classes(q).reshape(B, dil, nb, ATT_BLOCK, H, E)

    def windows(t):
        t = jnp.pad(classes(t), ((0, 0), (0, 0), (ATT_BLOCK, 0), (0, 0), (0, 0)))
        t = t.reshape(B, dil, nb + 1, ATT_BLOCK, H, E)
        return jnp.concatenate([t[:, :, :-1], t[:, :, 1:]], axis=3)

    kb = windows(k)
    vb = windows(v)
    qi = jnp.arange(ATT_BLOCK)[:, None]
    kj = jnp.arange(2 * ATT_BLOCK)[None, :]
    delta = qi + ATT_BLOCK - kj
    blk = jnp.arange(nb)[:, None, None]
    valid = (delta >= 0) & (delta <= wc) & (blk * ATT_BLOCK + kj - ATT_BLOCK >= 0)
    bias = rel_bias.astype(F32)[_rel_bucket(jnp.clip(delta, 0) * dil)]
    logits = (jnp.einsum('bdnqhe,bdnkhe->bdnhqk', qb, kb) * HEAD_DIM ** -0.5
              + jnp.transpose(bias, (2, 0, 1)))
    logits = jnp.where(valid[:, None], logits, -jnp.inf)
    mx = jnp.max(logits, axis=-1, keepdims=True)
    p = jnp.exp(logits - mx)
    s = jnp.sum(p, axis=-1)
    o = jnp.einsum('bdnhqk,bdnkhe->bdnqhe', p, vb) / jnp.swapaxes(s, -1, -2)[..., None]
    lse = jnp.swapaxes(mx[..., 0] + jnp.log(s), -1, -2)
    o = o.reshape(B, dil, lp, H, E)[:, :, :L].transpose(0, 2, 1, 3, 4).reshape(B, S, H, E)
    lse = lse.reshape(B, dil, lp, H)[:, :, :L].transpose(0, 2, 1, 3).reshape(B, S, H)
    return o, lse


def _dilated_branch_sample(q, k_all, v_all, rel_bias, window, dil, wbuf):
    T = q.shape[1]
    J = window // dil + 1
    ii = jnp.arange(T)[:, None]
    jj = jnp.arange(J)[None, :]
    idx = wbuf + ii - jj * dil
    valid = idx >= 0
    idx = jnp.maximum(idx, 0)
    kg = k_all[:, idx]
    vg = v_all[:, idx]
    bias = rel_bias.astype(F32)[_rel_bucket(jnp.arange(J) * dil)]
    logits = jnp.einsum('bthe,btjhe->bthj', q, kg) * HEAD_DIM ** -0.5 + bias.T
    logits = jnp.where(valid[None, :, None, :], logits, -jnp.inf)
    mx = jnp.max(logits, axis=-1, keepdims=True)
    p = jnp.exp(logits - mx)
    s = jnp.sum(p, axis=-1)
    o = jnp.einsum('bthj,btjhe->bthe', p, vg) / s[..., None]
    return o, mx[..., 0] + jnp.log(s)


def _combine_branches(outs, lses):
    wts = jax.nn.softmax(jnp.stack(lses, axis=0), axis=0)
    return jnp.sum(wts[..., None] * jnp.stack(outs, axis=0), axis=0)


def _layer(x, pe, i, W, state):
    B, L, _ = x.shape
    h = _rmsnorm(x, W['norm1_g'][i])
    z = h @ W['w_in'][i]
    mq, mk, mv, mo, mi, mf, aq, ak, av = _split_proj(z)

    if state is None:
        conv_buf = jnp.zeros((B, CONV_W - 1, 2 * W_M), z.dtype)
    else:
        conv_buf = state['conv']
    qk, new_conv = _short_conv(jnp.concatenate([mq, mk], axis=-1), conv_buf,
                               W['conv_w'][i], W['conv_b'][i])
    q_m = qk[..., :W_M].astype(F32).reshape(B, L, H_M, HEAD_DIM)
    k_m = qk[..., W_M:].astype(F32).reshape(B, L, H_M, HEAD_DIM) * HEAD_DIM ** -0.5
    v_m = mv.astype(F32).reshape(B, L, H_M, HEAD_DIM)
    gif = jnp.concatenate([mi, mf], axis=-1).astype(F32) + W['b_if'][i].astype(F32)
    ig = gif[..., :H_M]
    lf = jax.nn.log_sigmoid(gif[..., H_M:])
    if state is None:
        hm, C, n, m = _mlstm_prompt(q_m, k_m, v_m, ig, lf)
    else:
        hm, C, n, m = _mlstm_sample(q_m, k_m, v_m, ig, lf, state['C'].astype(F32),
                                    state['n'].astype(F32), state['m'].astype(F32))
    hm = hm * jax.nn.sigmoid(mo.astype(F32)).reshape(B, L, H_M, HEAD_DIM)
    hm = _rmsnorm(hm, W['mh_norm_g'][i].reshape(H_M, HEAD_DIM))

    q_a = _rmsnorm(aq.reshape(B, L, H_A, HEAD_DIM), W['q_norm_g'][i]).astype(F32)
    k_a = _rmsnorm(ak.reshape(B, L, H_A, HEAD_DIM), W['k_norm_g'][i]).astype(F32)
    v_a = av.reshape(B, L, H_A, HEAD_DIM).astype(F32)
    outs, lses = [], []
    if state is None:
        for (win, dil) in DIL_BRANCHES:
            o_r, l_r = _dilated_branch_prompt(q_a, k_a, v_a, W['rel_bias'], win, dil)
            outs.append(o_r)
            lses.append(l_r)
        keep = min(WINDOW_MAX, L)
        new_k = k_a[:, L - keep:]
        new_v = v_a[:, L - keep:]
    else:
        wbuf = state['k'].shape[1]
        k_all = jnp.concatenate([state['k'].astype(F32), k_a], axis=1)
        v_all = jnp.concatenate([state['v'].astype(F32), v_a], axis=1)
        for (win, dil) in DIL_BRANCHES:
            o_r, l_r = _dilated_branch_sample(q_a, k_all, v_all, W['rel_bias'], win, dil, wbuf)
            outs.append(o_r)
            lses.append(l_r)
        new_k = k_a
        new_v = v_a
    ha = _combine_branches(outs, lses)

    mix = jnp.concatenate([hm.reshape(B, L, W_M), ha.reshape(B, L, W_A)], axis=-1).astype(x.dtype)
    x = x + mix @ W['w_out'][i]

    h2 = _rmsnorm(x, W['norm2_g'][i])
    gu = h2 @ W['w_gu'][i]
    x = x + (jax.nn.silu(gu[..., :D_FF]) * gu[..., D_FF:]) @ W['w_down'][i]

    gate = jax.nn.sigmoid(_rmsnorm(x, W['ple_norm_g'][i]) @ W['w_pg'][i])
    x = x + gate * (pe @ W['w_pe'][i])
    return x, (new_k, new_v, C, n, m, new_conv)


def setup_inputs(seed: int = 0) -> dict:
    key = jax.random.key(seed)
    ks = jax.random.split(key, 32)
    wbuf = min(WINDOW_MAX, PAST_LEN)

    def nrm(k, shape, s):
        return jax.random.normal(k, shape, F32) * s

    b_if = jnp.concatenate([nrm(ks[20], (DEPTH, H_M), 0.1),
                            3.0 + nrm(ks[21], (DEPTH, H_M), 0.5)], axis=-1)
    return {
        'x_prompt': nrm(ks[0], (BATCH, SEQ, D_MODEL), 1.0),
        'x_sample': nrm(ks[1], (DEC_BATCH, DEC_SEQ, D_MODEL), 1.0),
        'p_prompt': nrm(ks[2], (DEPTH, BATCH, SEQ, D_PLE), 1.0),
        'p_sample': nrm(ks[3], (DEPTH, DEC_BATCH, DEC_SEQ, D_PLE), 1.0),
        'cache_attn_k': nrm(ks[4], (DEPTH, DEC_BATCH, wbuf, H_A, HEAD_DIM), 1.0),
        'cache_attn_v': nrm(ks[5], (DEPTH, DEC_BATCH, wbuf, H_A, HEAD_DIM), 1.0),
        'state_mlstm_C': nrm(ks[6], (DEPTH, DEC_BATCH, H_M, HEAD_DIM, HEAD_DIM), 0.1),
        'state_mlstm_n': nrm(ks[7], (DEPTH, DEC_BATCH, H_M, HEAD_DIM), 0.1),
        'state_mlstm_m': nrm(ks[8], (DEPTH, DEC_BATCH, H_M), 1.0),
        'state_conv': nrm(ks[9], (DEPTH, DEC_BATCH, CONV_W - 1, 2 * W_M), 1.0),
        'rel_bias': nrm(ks[10], (N_BUCKETS, H_A), 0.5),
        'norm1_g': 1.0 + nrm(ks[11], (DEPTH, D_MODEL), 0.01),
        'w_in': nrm(ks[12], (DEPTH, D_MODEL, D_IN), D_MODEL ** -0.5),
        'b_if': b_if,
        'conv_w': nrm(ks[13], (DEPTH, CONV_W, 2 * W_M), CONV_W ** -0.5),
        'conv_b': nrm(ks[14], (DEPTH, 2 * W_M), 0.01),
        'mh_norm_g': 1.0 + nrm(ks[15], (DEPTH, W_M), 0.01),
        'q_norm_g': 1.0 + nrm(ks[16], (DEPTH, HEAD_DIM), 0.01),
        'k_norm_g': 1.0 + nrm(ks[17], (DEPTH, HEAD_DIM), 0.01),
        'w_out': nrm(ks[18], (DEPTH, D_MODEL, D_MODEL), D_MODEL ** -0.5),
        'norm2_g': 1.0 + nrm(ks[19], (DEPTH, D_MODEL), 0.01),
        'w_gu': nrm(ks[22], (DEPTH, D_MODEL, 2 * D_FF), D_MODEL ** -0.5),
        'w_down': nrm(ks[23], (DEPTH, D_FF, D_MODEL), D_FF ** -0.5),
        'ple_norm_g': 1.0 + nrm(ks[24], (DEPTH, D_MODEL), 0.01),
        'w_pe': nrm(ks[25], (DEPTH, D_PLE, D_MODEL), D_PLE ** -0.5),
        'w_pg': nrm(ks[26], (DEPTH, D_MODEL, D_MODEL), D_MODEL ** -0.5),
    }


def reference(x_prompt, x_sample, p_prompt, p_sample, cache_attn_k, cache_attn_v,
              state_mlstm_C, state_mlstm_n, state_mlstm_m, state_conv,
              rel_bias, norm1_g, w_in, b_if, conv_w, conv_b, mh_norm_g, q_norm_g, k_norm_g,
              w_out, norm2_g, w_gu, w_down, ple_norm_g, w_pe, w_pg):
    W = dict(rel_bias=rel_bias, norm1_g=norm1_g, w_in=w_in, b_if=b_if, conv_w=conv_w,
             conv_b=conv_b, mh_norm_g=mh_norm_g, q_norm_g=q_norm_g, k_norm_g=k_norm_g,
             w_out=w_out, norm2_g=norm2_g, w_gu=w_gu, w_down=w_down,
             ple_norm_g=ple_norm_g, w_pe=w_pe, w_pg=w_pg)

    xp = x_prompt
    pk, pv, pC, pn, pm, pcv = [], [], [], [], [], []
    for i in range(DEPTH):
        xp, st = _layer(xp, p_prompt[i], i, W, None)
        pk.append(st[0]); pv.append(st[1]); pC.append(st[2])
        pn.append(st[3]); pm.append(st[4]); pcv.append(st[5])

    xs = x_sample
    sk, sv, sC, sn, sm, scv = [], [], [], [], [], []
    for i in range(DEPTH):
        layer_state = dict(k=cache_attn_k[i], v=cache_attn_v[i], C=state_mlstm_C[i],
                           n=state_mlstm_n[i], m=state_mlstm_m[i], conv=state_conv[i])
        xs, st = _layer(xs, p_sample[i], i, W, layer_state)
        sk.append(st[0]); sv.append(st[1]); sC.append(st[2])
        sn.append(st[3]); sm.append(st[4]); scv.append(st[5])

    y_prompt = xp
    y_sample = xs
    prompt_attn_k = jnp.stack(pk)
    prompt_attn_v = jnp.stack(pv)
    prompt_mlstm_C = jnp.stack(pC)
    prompt_mlstm_n = jnp.stack(pn)
    prompt_mlstm_m = jnp.stack(pm)
    prompt_conv = jnp.stack(pcv)
    sample_attn_k = jnp.stack(sk)
    sample_attn_v = jnp.stack(sv)
    sample_mlstm_C = jnp.stack(sC)
    sample_mlstm_n = jnp.stack(sn)
    sample_mlstm_m = jnp.stack(sm)
    sample_conv = jnp.stack(scv)
    return (y_prompt, y_sample, prompt_attn_k, prompt_attn_v, prompt_mlstm_C, prompt_mlstm_n,
            prompt_mlstm_m, prompt_conv, sample_attn_k, sample_attn_v, sample_mlstm_C,
            sample_mlstm_n, sample_mlstm_m, sample_conv)
```

```python
import functools
import math

import jax
import jax.numpy as jnp
from jax import lax
from jax.experimental import pallas as pl
from jax.experimental.pallas import tpu as pltpu

F32 = jnp.float32
BF16 = jnp.bfloat16

D_MODEL = 1024
HEAD_DIM = 64
H_M = 8
H_A = 8
W_M = H_M * HEAD_DIM
W_A = H_A * HEAD_DIM
CONV_W = 4
CHUNK = 128
DIL_BRANCHES = ((128, 1), (512, 4), (2048, 16))
N_BUCKETS = 32
MAX_DIST = 2048
D_FF = 2816
D_PLE = 256
EPS = 1e-6
NEG = -1e30

LANE = 128
SUBLANE = 8
VMEM_LIMIT_BYTES = 56 * 1024 * 1024

Z_QK = 0
Z_V = 2 * W_M
Z_O = 3 * W_M
Z_AQ = 4 * W_M
Z_AK = Z_AQ + W_A
Z_AV = Z_AK + W_A
Z_G = Z_AV + W_A
Z_W = Z_G + LANE

ROWS_IN = 256
ROWS_POST = 512
FF_BLOCK = D_FF // 2


def _params(*sem):
    return pltpu.CompilerParams(dimension_semantics=sem, vmem_limit_bytes=VMEM_LIMIT_BYTES)


def _rel_bucket(dist):
    max_exact = N_BUCKETS // 2
    df = jnp.maximum(dist, 1).astype(F32)
    large = max_exact + (jnp.log(df / max_exact) / math.log(MAX_DIST / max_exact)
                         * (N_BUCKETS - max_exact)).astype(jnp.int32)
    large = jnp.minimum(large, N_BUCKETS - 1)
    return jnp.where(dist < max_exact, dist, large)


def _split_dot(p, ones_bf16):
    hi = p.astype(BF16)
    lo = (p - hi.astype(F32)).astype(BF16)
    return (jnp.dot(hi, ones_bf16, preferred_element_type=F32)
            + jnp.dot(lo, ones_bf16, preferred_element_type=F32))


def _dot_nt(a, b):
    return lax.dot_general(a, b, (((1,), (1,)), ((), ())), preferred_element_type=F32)


def _in_proj_kernel(x_ref, g_ref, w_ref, qg_ref, kg_ref, seg_ref, z_ref):
    x = x_ref[...]
    ms = jnp.mean(x * x, axis=-1, keepdims=True)
    h = (x * lax.rsqrt(ms + EPS) * g_ref[...]).astype(BF16)
    z_ref[:, 0:Z_AQ] = jnp.dot(h, w_ref[:, 0:Z_AQ], preferred_element_type=F32)
    for off, gr in ((Z_AQ, qg_ref), (Z_AK, kg_ref)):
        a = jnp.dot(h, w_ref[:, off:off + W_A], preferred_element_type=F32)
        ss = _split_dot(a * a, seg_ref[...])
        z_ref[:, off:off + W_A] = a * lax.rsqrt(ss * (1.0 / HEAD_DIM) + EPS) * gr[...]
    z_ref[:, Z_AV:Z_W] = jnp.dot(h, w_ref[:, Z_AV:Z_W], preferred_element_type=F32)


def _in_proj(x, g, w, qg, kg, seg):
    n = x.shape[0]
    tm = min(ROWS_IN, n)
    const = lambda i: (0, 0)
    return pl.pallas_call(
        _in_proj_kernel,
        grid=(n // tm,),
        in_specs=[pl.BlockSpec((tm, D_MODEL), lambda i: (i, 0)),
                  pl.BlockSpec((1, D_MODEL), const),
                  pl.BlockSpec((D_MODEL, Z_W), const),
                  pl.BlockSpec((1, W_A), const),
                  pl.BlockSpec((1, W_A), const),
                  pl.BlockSpec((W_A, W_A), const)],
        out_specs=pl.BlockSpec((tm, Z_W), lambda i: (i, 0)),
        out_shape=jax.ShapeDtypeStruct((n, Z_W), F32),
        compiler_params=_params("parallel"),
        name="in_proj",
    )(x, g, w, qg, kg, seg)


def _cumsum_lanes(x):
    lane = lax.broadcasted_iota(jnp.int32, x.shape, 1)
    k = 1
    while k < x.shape[1]:
        x = x + jnp.where(lane >= k, pltpu.roll(x, k, axis=1), 0.0)
        k *= 2
    return x


def _mlstm_kernel(u_ref, v_ref, o_ref, g_ref, cprev_ref, c0_ref, n0_ref, m0_ref,
                  cw_ref, cb_ref, bif_ref, mhg_ref,
                  hm_ref, cout_ref, nout_ref, mout_ref,
                  ubuf, cbd, nst, mst, *, rows, nchunks):
    c = pl.program_id(1)

    @pl.when(c == 0)
    def _init():
        ubuf[...] = jnp.zeros(ubuf.shape, F32)
        ubuf[SUBLANE - (CONV_W - 1):SUBLANE, :] = cprev_ref[0]
        cbd[...] = c0_ref[0]
        nst[...] = n0_ref[0]
        mst[...] = m0_ref[0]

    ubuf[SUBLANE:SUBLANE + rows, :] = u_ref[...]
    y = cb_ref[...]
    for j in range(CONV_W):
        off = SUBLANE - (CONV_W - 1) + j
        y = y + ubuf[off:off + CHUNK, :] * cw_ref[j:j + 1, :]
    qk = y * jax.nn.sigmoid(y)

    row = lax.broadcasted_iota(jnp.int32, (CHUNK, LANE), 0)
    col = lax.broadcasted_iota(jnp.int32, (CHUNK, LANE), 1)
    lane_lo = col < HEAD_DIM
    lane_lo_row = lane_lo[0:1, :]
    causal = row >= col
    same_head = (row < HEAD_DIM) == lane_lo

    def pad_rows(t):
        if rows == CHUNK:
            return t
        return jnp.concatenate([t, jnp.zeros((CHUNK - rows, t.shape[1]), F32)], axis=0)

    gates = pad_rows(g_ref[...]) + bif_ref[...]
    logf = jnp.minimum(gates, 0.0) - jnp.log1p(jnp.exp(-jnp.abs(gates)))
    if rows == CHUNK:
        ig_c, lf_c = gates, logf
    else:
        ig_c = jnp.where(row < rows, gates, NEG)
        lf_c = jnp.where(row < rows, logf, 0.0)
    ig_t = ig_c.T
    b_t = _cumsum_lanes(lf_c.T)
    b_c = b_t.T

    v_all = pad_rows(v_ref[...])
    m_new = mst[...]
    last = slice(CHUNK - 1, CHUNK)
    for p in range(H_M // 2):
        sl = slice(p * LANE, (p + 1) * LANE)
        q_p = qk[:, sl]
        k_p = qk[:, W_M + p * LANE:W_M + (p + 1) * LANE] * HEAD_DIM ** -0.5
        k_bf = k_p.astype(BF16)
        v_p = v_all[:, sl]
        v_bf = v_p.astype(BF16)
        n_p = nst[:, sl]
        per_head = []
        for hh in range(2):
            h = 2 * p + hh
            sel = lane_lo if hh == 0 else jnp.logical_not(lane_lo)
            ig_row = ig_t[h:h + 1, :]
            b_row = b_t[H_M + h:H_M + h + 1, :]
            b_col = b_c[:, H_M + h:H_M + h + 1]
            ig_col = ig_c[:, h:h + 1]
            a_col = b_col + mst[0:1, h:h + 1]
            dm = jnp.where(causal, b_col - b_row + ig_row, NEG)
            m_t = jnp.maximum(a_col, jnp.max(dm, axis=1, keepdims=True))
            inter = jnp.exp(a_col - m_t)
            qm = jnp.where(sel, q_p, 0.0).astype(BF16)
            s = _dot_nt(qm, k_bf) * jnp.exp(dm - m_t)
            ssum = jnp.sum(s, axis=1, keepdims=True)
            sv = jnp.dot(s.astype(BF16), v_bf, preferred_element_type=F32)
            qn = jnp.sum(jnp.where(sel, q_p * n_p, 0.0), axis=1, keepdims=True)
            den = inter * qn + ssum
            dnm = jnp.maximum(jnp.abs(den), jnp.exp(-m_t))
            m_last = m_t[last, :]
            w_col = jnp.exp(b_col[last, :] - b_col + ig_col - m_last)
            decay = jnp.exp(a_col[last, :] - m_last)
            m_new = jnp.where(col[0:1, :] == h, m_last, m_new)
            per_head.append((inter, sv, dnm, w_col, decay))
        (i0, sv0, d0, w0, dc0), (i1, sv1, d1, w1, dc1) = per_head
        c_old = cbd[p]
        qc = _dot_nt(q_p.astype(BF16), c_old.astype(BF16))
        num = jnp.where(lane_lo, i0, i1) * qc + jnp.where(lane_lo, sv0, sv1)
        hv = (num / jnp.where(lane_lo, d0, d1))[0:rows, :]
        hv = hv * jax.nn.sigmoid(o_ref[:, sl])
        sq = hv * hv
        lo_r = lane_lo[0:rows, :]
        ms0 = jnp.sum(jnp.where(lo_r, sq, 0.0), axis=1, keepdims=True) * (1.0 / HEAD_DIM)
        ms1 = jnp.sum(jnp.where(lo_r, 0.0, sq), axis=1, keepdims=True) * (1.0 / HEAD_DIM)
        rs = jnp.where(lo_r, lax.rsqrt(ms0 + EPS), lax.rsqrt(ms1 + EPS))
        hm_ref[:, sl] = hv * rs * mhg_ref[:, sl]

        w_p = jnp.where(lane_lo, w0, w1)
        dec_p = jnp.where(lane_lo_row, dc0, dc1)
        c_upd = jnp.dot((v_p * w_p).T.astype(BF16), k_bf, preferred_element_type=F32)
        cbd[p] = dec_p * c_old + jnp.where(same_head, c_upd, 0.0)
        nst[:, sl] = dec_p * n_p + jnp.sum(k_p * w_p, axis=0, keepdims=True)
    mst[...] = m_new

    if nchunks > 1:
        ubuf[0:SUBLANE, :] = ubuf[CHUNK:CHUNK + SUBLANE, :]

    @pl.when(c == nchunks - 1)
    def _fin():
        cout_ref[0] = cbd[...]
        nout_ref[0] = nst[...]
        mout_ref[0] = mst[...]


def _mlstm(z, cprev, c0, n0, m0, cw, cb, bif, mhg, *, nseq, rows, nchunks):
    n = z.shape[0]
    tok = lambda b, c: b * nchunks + c
    const2 = lambda b, c: (0, 0)
    kern = functools.partial(_mlstm_kernel, rows=rows, nchunks=nchunks)
    npair = H_M // 2
    return pl.pallas_call(
        kern,
        grid=(nseq, nchunks),
        in_specs=[pl.BlockSpec((rows, 2 * W_M), lambda b, c: (tok(b, c), Z_QK // (2 * W_M))),
                  pl.BlockSpec((rows, W_M), lambda b, c: (tok(b, c), Z_V // W_M)),
                  pl.BlockSpec((rows, W_M), lambda b, c: (tok(b, c), Z_O // W_M)),
                  pl.BlockSpec((rows, LANE), lambda b, c: (tok(b, c), Z_G // LANE)),
                  pl.BlockSpec((1, CONV_W - 1, 2 * W_M), lambda b, c: (b, 0, 0)),
                  pl.BlockSpec((1, npair, LANE, LANE), lambda b, c: (b, 0, 0, 0)),
                  pl.BlockSpec((1, 1, W_M), lambda b, c: (b, 0, 0)),
                  pl.BlockSpec((1, 1, LANE), lambda b, c: (b, 0, 0)),
                  pl.BlockSpec((CONV_W, 2 * W_M), const2),
                  pl.BlockSpec((1, 2 * W_M), const2),
                  pl.BlockSpec((1, LANE), const2),
                  pl.BlockSpec((1, W_M), const2)],
        out_specs=[pl.BlockSpec((rows, W_M), lambda b, c: (tok(b, c), 0)),
                   pl.BlockSpec((1, npair, LANE, LANE), lambda b, c: (b, 0, 0, 0)),
                   pl.BlockSpec((1, 1, W_M), lambda b, c: (b, 0, 0)),
                   pl.BlockSpec((1, 1, LANE), lambda b, c: (b, 0, 0))],
        out_shape=[jax.ShapeDtypeStruct((n, W_M), F32),
                   jax.ShapeDtypeStruct((nseq, npair, LANE, LANE), F32),
                   jax.ShapeDtypeStruct((nseq, 1, W_M), F32),
                   jax.ShapeDtypeStruct((nseq, 1, LANE), F32)],
        scratch_shapes=[pltpu.VMEM((CHUNK + SUBLANE, 2 * W_M), F32),
                        pltpu.VMEM((npair, LANE, LANE), F32),
                        pltpu.VMEM((1, W_M), F32),
                        pltpu.VMEM((1, LANE), F32)],
        compiler_params=_params("parallel", "arbitrary"),
        name="mlstm",
    )(z, z, z, z, cprev, c0, n0, m0, cw, cb, bif, mhg)


def _to_blockdiag(c):
    lead = c.shape[:-3]
    cp = c.reshape(lead + (H_M // 2, 2, HEAD_DIM, HEAD_DIM))
    zero = jnp.zeros_like(cp[..., 0, :, :])
    top = jnp.concatenate([cp[..., 0, :, :], zero], axis=-1)
    bot = jnp.concatenate([zero, cp[..., 1, :, :]], axis=-1)
    return jnp.concatenate([top, bot], axis=-2)


def _from_blockdiag(cbd):
    lead = cbd.shape[:-3]
    c0 = cbd[..., :HEAD_DIM, :HEAD_DIM]
    c1 = cbd[..., HEAD_DIM:, HEAD_DIM:]
    return jnp.stack([c0, c1], axis=-3).reshape(lead + (H_M, HEAD_DIM, HEAD_DIM))


def _attn_prompt_kernel(q_ref, k_ref, v_ref, tab_ref, out_ref, obuf, lbuf, *, seq):
    col = lax.broadcasted_iota(jnp.int32, (CHUNK, LANE), 1)
    lane_lo = col < HEAD_DIM
    scale = HEAD_DIM ** -0.5

    def block(bi, dil, start, with_prev):
        def rows(st):
            return pl.ds(st, CHUNK) if dil == 1 else pl.ds(st, CHUNK, stride=dil)

        qb = q_ref[rows(start), :]
        kw = k_ref[rows(start), :]
        vw = v_ref[rows(start), :]
        if with_prev:
            prev = start - dil * CHUNK
            kw = jnp.concatenate([k_ref[rows(prev), :], kw], axis=0)
            vw = jnp.concatenate([v_ref[rows(prev), :], vw], axis=0)
        kw = kw.astype(BF16)
        vw = vw.astype(BF16)
        res = []
        for hh in range(2):
            sel = lane_lo if hh == 0 else jnp.logical_not(lane_lo)
            qm = jnp.where(sel, qb, 0.0).astype(BF16)
            bias = tab_ref[bi, hh] if with_prev else tab_ref[bi, hh, :, CHUNK:]
            lg = _dot_nt(qm, kw) * scale + bias
            mx = jnp.max(lg, axis=1, keepdims=True)
            pr = jnp.exp(lg - mx)
            sm = jnp.sum(pr, axis=1, keepdims=True)
            pv = jnp.dot(pr.astype(BF16), vw, preferred_element_type=F32)
            res.append((pv / sm, mx + jnp.log(sm)))
        obuf[bi, rows(start), :] = jnp.where(lane_lo, res[0][0], res[1][0])
        lbuf[bi, rows(start), :] = jnp.where(lane_lo, res[0][1], res[1][1])

    for bi, (win, dil) in enumerate(DIL_BRANCHES):
        assert win // dil == CHUNK
        nblk = seq // (dil * CHUNK)

        if dil == 1:
            block(bi, dil, 0, False)
        else:
            def first(cls, carry, bi=bi, dil=dil):
                block(bi, dil, cls, False)
                return carry
            lax.fori_loop(0, dil, first, 0)

        if nblk > 1:
            def rest(i, carry, bi=bi, dil=dil):
                cls = i % dil
                blk = 1 + i // dil
                start = cls + dil * CHUNK * blk
                if dil == 1:
                    start = pl.multiple_of(start, CHUNK)
                block(bi, dil, start, True)
                return carry
            lax.fori_loop(0, dil * (nblk - 1), rest, 0)

    step = 2 * CHUNK

    def combine(i, carry):
        r = pl.ds(pl.multiple_of(i * step, step), step)
        l0, l1, l2 = lbuf[0, r, :], lbuf[1, r, :], lbuf[2, r, :]
        mx = jnp.maximum(jnp.maximum(l0, l1), l2)
        e0, e1, e2 = jnp.exp(l0 - mx), jnp.exp(l1 - mx), jnp.exp(l2 - mx)
        den = e0 + e1 + e2
        out_ref[r, :] = ((e0 / den) * obuf[0, r, :] + (e1 / den) * obuf[1, r, :]
                         + (e2 / den) * obuf[2, r, :])
        return carry
    lax.fori_loop(0, seq // step, combine, 0)


def _attn_prompt(z, tab, *, nseq, seq):
    n = z.shape[0]
    npair = H_A // 2
    nb = len(DIL_BRANCHES)
    kern = functools.partial(_attn_prompt_kernel, seq=seq)
    return pl.pallas_call(
        kern,
        grid=(nseq, npair),
        in_specs=[pl.BlockSpec((seq, LANE), lambda b, p: (b, Z_AQ // LANE + p)),
                  pl.BlockSpec((seq, LANE), lambda b, p: (b, Z_AK // LANE + p)),
                  pl.BlockSpec((seq, LANE), lambda b, p: (b, Z_AV // LANE + p)),
                  pl.BlockSpec((nb, 2, CHUNK, 2 * CHUNK), lambda b, p: (0, p, 0, 0))],
        out_specs=pl.BlockSpec((seq, LANE), lambda b, p: (b, p)),
        out_shape=jax.ShapeDtypeStruct((n, W_A), F32),
        scratch_shapes=[pltpu.VMEM((nb, seq, LANE), F32),
                        pltpu.VMEM((nb, seq, LANE), F32)],
        compiler_params=_params("parallel", "parallel"),
        name="attn_prompt",
    )(z, z, z, tab)


def _prompt_bias_table(rel_bias):
    qi = jnp.arange(CHUNK)[:, None]
    kj = jnp.arange(2 * CHUNK)[None, :]
    delta = qi + CHUNK - kj
    tabs = []
    for win, dil in DIL_BRANCHES:
        wc = win // dil
        valid = (delta >= 0) & (delta <= wc)
        bias = rel_bias.astype(F32)[_rel_bucket(jnp.clip(delta, 0) * dil)]
        tabs.append(jnp.where(valid[None], jnp.transpose(bias, (2, 0, 1)), NEG))
    return jnp.stack(tabs)


TAB_PAD = SUBLANE


def _attn_sample_kernel(li_ref, q_ref, kn_ref, vn_ref, kc_ref, vc_ref, tab_ref, out_ref,
                        *, tokens, wbuf):
    del li_ref
    scale = HEAD_DIM ** -0.5
    for ii in range(tokens):
        q_t = q_ref[0, ii]
        outs, lses = [], []
        for bi, (win, dil) in enumerate(DIL_BRANCHES):
            nk = win // dil
            fl = ii // dil
            start = wbuf - nk * dil + ii % dil
            idx = pl.ds(start, nk) if dil == 1 else pl.ds(start, nk, stride=dil)
            kc = kc_ref[0, 0, idx]
            vc = vc_ref[0, 0, idx]
            lg = (jnp.sum(kc * q_t[None], axis=-1, keepdims=True) * scale
                  + tab_ref[bi, TAB_PAD - fl:TAB_PAD - fl + nk])
            mx = jnp.max(lg, axis=0)
            fresh = []
            for jj in range(fl + 1):
                i2 = ii - dil * jj
                ln = (jnp.sum(kn_ref[0, i2] * q_t, axis=-1, keepdims=True) * scale
                      + tab_ref[bi, TAB_PAD + nk - jj])
                fresh.append((ln, i2))
                mx = jnp.maximum(mx, ln)
            pr = jnp.exp(lg - mx[None])
            sm = jnp.sum(pr, axis=0)
            acc = jnp.sum(pr * vc, axis=0)
            for ln, i2 in fresh:
                pn = jnp.exp(ln - mx)
                sm = sm + pn
                acc = acc + pn * vn_ref[0, i2]
            outs.append(acc / sm)
            lses.append(mx + jnp.log(sm))
        mx = jnp.maximum(jnp.maximum(lses[0], lses[1]), lses[2])
        es = [jnp.exp(l - mx) for l in lses]
        den = es[0] + es[1] + es[2]
        out_ref[0, ii] = ((es[0] / den) * outs[0] + (es[1] / den) * outs[1]
                          + (es[2] / den) * outs[2])


def _attn_sample(layer, q, kn, vn, cache_k, cache_v, tab):
    nseq, tokens = q.shape[0], q.shape[1]
    wbuf = cache_k.shape[2]
    assert wbuf >= max(w for w, _ in DIL_BRANCHES)
    nb = len(DIL_BRANCHES)
    new_spec = pl.BlockSpec((1, tokens, H_A, HEAD_DIM), lambda b, li: (b, 0, 0, 0))
    cache_spec = pl.BlockSpec((1, 1, wbuf, H_A, HEAD_DIM), lambda b, li: (li[0], b, 0, 0, 0))
    kern = functools.partial(_attn_sample_kernel, tokens=tokens, wbuf=wbuf)
    return pl.pallas_call(
        kern,
        grid_spec=pltpu.PrefetchScalarGridSpec(
            num_scalar_prefetch=1,
            grid=(nseq,),
            in_specs=[new_spec, new_spec, new_spec, cache_spec, cache_spec,
                      pl.BlockSpec((nb, TAB_PAD + CHUNK + 1, H_A, HEAD_DIM),
                                   lambda b, li: (0, 0, 0, 0))],
            out_specs=new_spec),
        out_shape=jax.ShapeDtypeStruct((nseq, tokens, H_A, HEAD_DIM), F32),
        compiler_params=_params("parallel"),
        name="attn_sample",
    )(layer, q, kn, vn, cache_k, cache_v, tab)


def _sample_bias_table(rel_bias):
    tabs = []
    for win, dil in DIL_BRANCHES:
        nk = win // dil
        steps = nk - jnp.arange(nk + 1)
        bias = rel_bias.astype(F32)[_rel_bucket(steps * dil)]
        bias = jnp.concatenate([jnp.full((TAB_PAD, H_A), NEG, F32), bias], axis=0)
        tabs.append(jnp.broadcast_to(bias[:, :, None], bias.shape + (HEAD_DIM,)))
    return jnp.stack(tabs)


def _out_proj_kernel(x_ref, hm_ref, ha_ref, wo_ref, g2_ref, x1_ref, h2_ref):
    x1 = (x_ref[...]
          + jnp.dot(hm_ref[...].astype(BF16), wo_ref[0:W_M, :], preferred_element_type=F32)
          + jnp.dot(ha_ref[...].astype(BF16), wo_ref[W_M:, :], preferred_element_type=F32))
    x1_ref[...] = x1
    ms = jnp.mean(x1 * x1, axis=-1, keepdims=True)
    h2_ref[...] = (x1 * lax.rsqrt(ms + EPS) * g2_ref[...]).astype(BF16)


def _out_proj(x, hm, ha, wo, g2):
    n = x.shape[0]
    tm = min(ROWS_POST, n)
    const = lambda i: (0, 0)
    row = lambda i: (i, 0)
    return pl.pallas_call(
        _out_proj_kernel,
        grid=(n // tm,),
        in_specs=[pl.BlockSpec((tm, D_MODEL), row),
                  pl.BlockSpec((tm, W_M), row),
                  pl.BlockSpec((tm, W_A), row),
                  pl.BlockSpec((D_MODEL, D_MODEL), const),
                  pl.BlockSpec((1, D_MODEL), const)],
        out_specs=[pl.BlockSpec((tm, D_MODEL), row), pl.BlockSpec((tm, D_MODEL), row)],
        out_shape=[jax.ShapeDtypeStruct((n, D_MODEL), F32),
                   jax.ShapeDtypeStruct((n, D_MODEL), BF16)],
        compiler_params=_params("parallel"),
        name="out_proj",
    )(x, hm, ha, wo, g2)


def _mlp_ple_kernel(x1_ref, h2_ref, pe_ref, wg_ref, wu_ref, wd_ref, g3_ref, wpg_ref, wpe_ref,
                    out_ref, acc, *, nff):
    j = pl.program_id(1)

    @pl.when(j == 0)
    def _zero():
        acc[...] = jnp.zeros(acc.shape, F32)

    h2 = h2_ref[...]
    gate = jnp.dot(h2, wg_ref[...], preferred_element_type=F32)
    up = jnp.dot(h2, wu_ref[...], preferred_element_type=F32)
    act = (gate * jax.nn.sigmoid(gate) * up).astype(BF16)
    acc[...] += jnp.dot(act, wd_ref[...], preferred_element_type=F32)

    @pl.when(j == nff - 1)
    def _fin():
        x2 = x1_ref[...] + acc[...]
        ms = jnp.mean(x2 * x2, axis=-1, keepdims=True)
        h3 = (x2 * lax.rsqrt(ms + EPS) * g3_ref[...]).astype(BF16)
        pg = jax.nn.sigmoid(jnp.dot(h3, wpg_ref[...], preferred_element_type=F32))
        pp = jnp.dot(pe_ref[...].astype(BF16), wpe_ref[...], preferred_element_type=F32)
        out_ref[...] = x2 + pg * pp


def _mlp_ple(x1, h2, pe, wgu, wd, g3, wpg, wpe):
    n = x1.shape[0]
    tm = min(ROWS_POST, n)
    nff = D_FF // FF_BLOCK
    const = lambda i, j: (0, 0)
    row = lambda i, j: (i, 0)
    kern = functools.partial(_mlp_ple_kernel, nff=nff)
    return pl.pallas_call(
        kern,
        grid=(n // tm, nff),
        in_specs=[pl.BlockSpec((tm, D_MODEL), row),
                  pl.BlockSpec((tm, D_MODEL), row),
                  pl.BlockSpec((tm, D_PLE), row),
                  pl.BlockSpec((D_MODEL, FF_BLOCK), lambda i, j: (0, j)),
                  pl.BlockSpec((D_MODEL, FF_BLOCK), lambda i, j: (0, nff + j)),
                  pl.BlockSpec((FF_BLOCK, D_MODEL), lambda i, j: (j, 0)),
                  pl.BlockSpec((1, D_MODEL), const),
                  pl.BlockSpec((D_MODEL, D_MODEL), const),
                  pl.BlockSpec((D_PLE, D_MODEL), const)],
        out_specs=pl.BlockSpec((tm, D_MODEL), row),
        out_shape=jax.ShapeDtypeStruct((n, D_MODEL), F32),
        scratch_shapes=[pltpu.VMEM((tm, D_MODEL), F32)],
        compiler_params=_params("parallel", "arbitrary"),
        name="mlp_ple",
    )(x1, h2, pe, wgu, wgu, wd, g3, wpg, wpe)


def _mixer_tail(x, z, hm, ha, pe, lw):
    x1, h2 = _out_proj(x, hm, ha, lw["w_out"], lw["norm2_g"])
    return _mlp_ple(x1, h2, pe, lw["w_gu"], lw["w_down"], lw["ple_norm_g"], lw["w_pg"], lw["w_pe"])


def _layer(carry, lw, *, batch, seq, dec_batch, dec_seq, cache_k, cache_v, tab_p, tab_s, seg):
    xp, xs = carry

    def proj(x):
        return _in_proj(x, lw["norm1_g"], lw["w_in"], lw["q_norm_g"], lw["k_norm_g"], seg)

    def mlstm(z, cprev, c0, n0, m0, nseq, rows, nchunks):
        return _mlstm(z, cprev, c0, n0, m0, lw["conv_w"], lw["conv_b"], lw["b_if"], lw["mh_norm_g"],
                      nseq=nseq, rows=rows, nchunks=nchunks)

    zp = proj(xp)
    hm, pc, pn, pm = mlstm(zp,
                           jnp.zeros((batch, CONV_W - 1, 2 * W_M), F32),
                           jnp.zeros((batch, H_M // 2, LANE, LANE), F32),
                           jnp.zeros((batch, 1, W_M), F32),
                           jnp.zeros((batch, 1, LANE), F32),
                           batch, CHUNK, seq // CHUNK)
    ha = _attn_prompt(zp, tab_p, nseq=batch, seq=seq)
    xp_new = _mixer_tail(xp, zp, hm, ha, lw["p_prompt"], lw)
    zp3 = zp.reshape(batch, seq, Z_W)
    keep = min(max(w for w, _ in DIL_BRANCHES), seq)
    p_out = (zp3[:, seq - keep:, Z_AK:Z_AK + W_A].reshape(batch, keep, H_A, HEAD_DIM),
             zp3[:, seq - keep:, Z_AV:Z_AV + W_A].reshape(batch, keep, H_A, HEAD_DIM),
             _from_blockdiag(pc),
             pn.reshape(batch, H_M, HEAD_DIM),
             pm[:, 0, :H_M],
             zp3[:, seq - (CONV_W - 1):, Z_QK:Z_QK + 2 * W_M])

    zs = proj(xs)
    hm, sc, sn, sm = mlstm(zs, lw["state_conv"], lw["state_C"], lw["state_n"], lw["state_m"],
                           dec_batch, dec_seq, 1)
    zs3 = zs.reshape(dec_batch, dec_seq, Z_W)
    heads = (dec_batch, dec_seq, H_A, HEAD_DIM)
    q_s = zs3[:, :, Z_AQ:Z_AQ + W_A].reshape(heads)
    k_s = zs3[:, :, Z_AK:Z_AK + W_A].reshape(heads)
    v_s = zs3[:, :, Z_AV:Z_AV + W_A].reshape(heads)
    ha = _attn_sample(lw["layer"], q_s, k_s, v_s, cache_k, cache_v, tab_s)
    xs_new = _mixer_tail(xs, zs, hm, ha.reshape(dec_batch * dec_seq, W_A), lw["p_sample"], lw)
    conv_rows = jnp.concatenate([lw["state_conv"], zs3[:, :, Z_QK:Z_QK + 2 * W_M]], axis=1)
    s_out = (k_s, v_s,
             _from_blockdiag(sc),
             sn.reshape(dec_batch, H_M, HEAD_DIM),
             sm[:, 0, :H_M],
             conv_rows[:, -(CONV_W - 1):])
    return (xp_new, xs_new), (p_out, s_out)


def kernel(x_prompt, x_sample, p_prompt, p_sample, cache_attn_k, cache_attn_v, state_mlstm_C, state_mlstm_n, state_mlstm_m, state_conv, rel_bias, norm1_g, w_in, b_if, conv_w, conv_b, mh_norm_g, q_norm_g, k_norm_g, w_out, norm2_g, w_gu, w_down, ple_norm_g, w_pe, w_pg):
    batch, seq, _ = x_prompt.shape
    dec_batch, dec_seq, _ = x_sample.shape
    depth = w_in.shape[0]
    assert seq % (CHUNK * max(d for _, d in DIL_BRANCHES)) == 0

    n_m = 4 * W_M
    n_g = 2 * H_M
    w_in_z = jnp.concatenate(
        [w_in[:, :, :n_m], w_in[:, :, n_m + n_g:], w_in[:, :, n_m:n_m + n_g],
         jnp.zeros((depth, D_MODEL, LANE - n_g), w_in.dtype)], axis=-1).astype(BF16)

    head_of_lane = jnp.arange(W_A) // HEAD_DIM
    seg = (head_of_lane[:, None] == head_of_lane[None, :]).astype(BF16)
    tab_p = _prompt_bias_table(rel_bias)
    tab_s = _sample_bias_table(rel_bias)

    row = lambda a: a[:, None, :]
    per_layer = dict(
        layer=jnp.arange(depth, dtype=jnp.int32)[:, None],
        norm1_g=row(norm1_g), w_in=w_in_z,
        q_norm_g=row(jnp.tile(q_norm_g, (1, H_A))), k_norm_g=row(jnp.tile(k_norm_g, (1, H_A))),
        conv_w=conv_w, conv_b=row(conv_b),
        b_if=row(jnp.pad(b_if, ((0, 0), (0, LANE - n_g)))),
        mh_norm_g=row(mh_norm_g),
        w_out=w_out.astype(BF16), norm2_g=row(norm2_g),
        w_gu=w_gu.astype(BF16), w_down=w_down.astype(BF16),
        ple_norm_g=row(ple_norm_g), w_pg=w_pg.astype(BF16), w_pe=w_pe.astype(BF16),
        p_prompt=p_prompt.reshape(depth, batch * seq, D_PLE),
        p_sample=p_sample.reshape(depth, dec_batch * dec_seq, D_PLE),
        state_conv=state_conv,
        state_C=_to_blockdiag(state_mlstm_C),
        state_n=state_mlstm_n.reshape(depth, dec_batch, 1, W_M),
        state_m=jnp.pad(state_mlstm_m, ((0, 0), (0, 0), (0, LANE - H_M)))[:, :, None, :],
    )
    step = functools.partial(_layer, batch=batch, seq=seq, dec_batch=dec_batch, dec_seq=dec_seq,
                             cache_k=cache_attn_k, cache_v=cache_attn_v,
                             tab_p=tab_p, tab_s=tab_s, seg=seg)
    carry = (x_prompt.reshape(batch * seq, D_MODEL), x_sample.reshape(dec_batch * dec_seq, D_MODEL))
    (xp, xs), (p_out, s_out) = lax.scan(step, carry, per_layer)
    return (xp.reshape(batch, seq, D_MODEL), xs.reshape(dec_batch, dec_seq, D_MODEL),
            *p_out, *s_out)
```

```python
import functools
import math

import jax
import jax.numpy as jnp
from jax import lax
from jax.experimental import pallas as pl
from jax.experimental.pallas import tpu as pltpu

F32 = jnp.float32
BF16 = jnp.bfloat16

D_MODEL = 1024
HEAD_DIM = 64
H_M = 8
H_A = 8
W_M = H_M * HEAD_DIM
W_A = H_A * HEAD_DIM
CONV_W = 4
CHUNK = 128
DIL_BRANCHES = ((128, 1), (512, 4), (2048, 16))
N_BUCKETS = 32
MAX_DIST = 2048
D_FF = 2816
D_PLE = 256
EPS = 1e-6
NEG = -1e30

LANE = 128
SUBLANE = 8
VMEM_LIMIT_BYTES = 56 * 1024 * 1024

Z_QK = 0
Z_V = 2 * W_M
Z_O = 3 * W_M
Z_AQ = 4 * W_M
Z_AK = Z_AQ + W_A
Z_AV = Z_AK + W_A
Z_G = Z_AV + W_A
Z_W = Z_G + LANE

ROWS_IN = 256
ROWS_POST = 512
FF_BLOCK = D_FF // 2


def _params(*sem):
    return pltpu.CompilerParams(dimension_semantics=sem, vmem_limit_bytes=VMEM_LIMIT_BYTES)


def _rel_bucket(dist):
    max_exact = N_BUCKETS // 2
    df = jnp.maximum(dist, 1).astype(F32)
    large = max_exact + (jnp.log(df / max_exact) / math.log(MAX_DIST / max_exact)
                         * (N_BUCKETS - max_exact)).astype(jnp.int32)
    large = jnp.minimum(large, N_BUCKETS - 1)
    return jnp.where(dist < max_exact, dist, large)


def _split_dot(p, ones_bf16):
    hi = p.astype(BF16)
    lo = (p - hi.astype(F32)).astype(BF16)
    return (jnp.dot(hi, ones_bf16, preferred_element_type=F32)
            + jnp.dot(lo, ones_bf16, preferred_element_type=F32))


def _dot_nt(a, b):
    return lax.dot_general(a, b, (((1,), (1,)), ((), ())), preferred_element_type=F32)


def _in_proj_kernel(x_ref, g_ref, w_ref, qg_ref, kg_ref, seg_ref, z_ref):
    x = x_ref[...]
    ms = jnp.mean(x * x, axis=-1, keepdims=True)
    h = (x * lax.rsqrt(ms + EPS) * g_ref[...]).astype(BF16)
    z_ref[:, 0:Z_AQ] = jnp.dot(h, w_ref[:, 0:Z_AQ], preferred_element_type=F32)
    for off, gr, mult in ((Z_AQ, qg_ref, HEAD_DIM ** -0.5), (Z_AK, kg_ref, None)):
        a = jnp.dot(h, w_ref[:, off:off + W_A], preferred_element_type=F32)
        ss = _split_dot(a * a, seg_ref[...])
        a = a * lax.rsqrt(ss * (1.0 / HEAD_DIM) + EPS) * gr[...]
        z_ref[:, off:off + W_A] = a if mult is None else a * mult
    z_ref[:, Z_AV:Z_W] = jnp.dot(h, w_ref[:, Z_AV:Z_W], preferred_element_type=F32)


def _in_proj(x, g, w, qg, kg, seg):
    n = x.shape[0]
    tm = min(ROWS_IN, n)
    const = lambda i: (0, 0)
    return pl.pallas_call(
        _in_proj_kernel,
        grid=(n // tm,),
        in_specs=[pl.BlockSpec((tm, D_MODEL), lambda i: (i, 0)),
                  pl.BlockSpec((1, D_MODEL), const),
                  pl.BlockSpec((D_MODEL, Z_W), const),
                  pl.BlockSpec((1, W_A), const),
                  pl.BlockSpec((1, W_A), const),
                  pl.BlockSpec((W_A, W_A), const)],
        out_specs=pl.BlockSpec((tm, Z_W), lambda i: (i, 0)),
        out_shape=jax.ShapeDtypeStruct((n, Z_W), F32),
        compiler_params=_params("parallel"),
        name="in_proj",
    )(x, g, w, qg, kg, seg)


def _cumsum_lanes(x):
    lane = lax.broadcasted_iota(jnp.int32, x.shape, 1)
    k = 1
    while k < x.shape[1]:
        x = x + jnp.where(lane >= k, pltpu.roll(x, k, axis=1), 0.0)
        k *= 2
    return x


def _mlstm_kernel(u_ref, v_ref, o_ref, g_ref, cprev_ref, c0_ref, n0_ref, m0_ref,
                  cw_ref, cb_ref, bif_ref, mhg_ref,
                  hm_ref, cout_ref, nout_ref, mout_ref,
                  ubuf, cbd, nst, mst, *, rows, nchunks):
    c = pl.program_id(1)

    @pl.when(c == 0)
    def _init():
        ubuf[...] = jnp.zeros(ubuf.shape, F32)
        ubuf[SUBLANE - (CONV_W - 1):SUBLANE, :] = cprev_ref[0]
        cbd[...] = c0_ref[0]
        nst[...] = n0_ref[0]
        mst[...] = m0_ref[0]

    ubuf[SUBLANE:SUBLANE + rows, :] = u_ref[...]
    y = cb_ref[...]
    for j in range(CONV_W):
        off = SUBLANE - (CONV_W - 1) + j
        y = y + ubuf[off:off + CHUNK, :] * cw_ref[j:j + 1, :]
    qk = y * jax.nn.sigmoid(y)

    row = lax.broadcasted_iota(jnp.int32, (CHUNK, LANE), 0)
    col = lax.broadcasted_iota(jnp.int32, (CHUNK, LANE), 1)
    lane_lo = col < HEAD_DIM
    lane_lo_row = lane_lo[0:1, :]
    causal = row >= col
    same_head = (row < HEAD_DIM) == lane_lo

    def pad_rows(t):
        if rows == CHUNK:
            return t
        return jnp.concatenate([t, jnp.zeros((CHUNK - rows, t.shape[1]), F32)], axis=0)

    gates = pad_rows(g_ref[...]) + bif_ref[...]
    logf = jnp.minimum(gates, 0.0) - jnp.log1p(jnp.exp(-jnp.abs(gates)))
    if rows == CHUNK:
        ig_c, lf_c = gates, logf
    else:
        ig_c = jnp.where(row < rows, gates, NEG)
        lf_c = jnp.where(row < rows, logf, 0.0)
    ig_t = ig_c.T
    b_t = _cumsum_lanes(lf_c.T)
    b_c = b_t.T

    v_all = pad_rows(v_ref[...])
    m_new = mst[...]
    last = slice(CHUNK - 1, CHUNK)
    for p in range(H_M // 2):
        sl = slice(p * LANE, (p + 1) * LANE)
        q_p = qk[:, sl]
        k_p = qk[:, W_M + p * LANE:W_M + (p + 1) * LANE] * HEAD_DIM ** -0.5
        k_bf = k_p.astype(BF16)
        v_p = v_all[:, sl]
        v_bf = v_p.astype(BF16)
        n_p = nst[:, sl]
        per_head = []
        for hh in range(2):
            h = 2 * p + hh
            sel = lane_lo if hh == 0 else jnp.logical_not(lane_lo)
            ig_row = ig_t[h:h + 1, :]
            b_row = b_t[H_M + h:H_M + h + 1, :]
            b_col = b_c[:, H_M + h:H_M + h + 1]
            ig_col = ig_c[:, h:h + 1]
            a_col = b_col + mst[0:1, h:h + 1]
            dm = jnp.where(causal, b_col - b_row + ig_row, NEG)
            m_t = jnp.maximum(a_col, jnp.max(dm, axis=1, keepdims=True))
            inter = jnp.exp(a_col - m_t)
            qm = jnp.where(sel, q_p, 0.0).astype(BF16)
            s = _dot_nt(qm, k_bf) * jnp.exp(dm - m_t)
            ssum = jnp.sum(s, axis=1, keepdims=True)
            sv = jnp.dot(s.astype(BF16), v_bf, preferred_element_type=F32)
            qn = jnp.sum(jnp.where(sel, q_p * n_p, 0.0), axis=1, keepdims=True)
            den = inter * qn + ssum
            dnm = jnp.maximum(jnp.abs(den), jnp.exp(-m_t))
            m_last = m_t[last, :]
            w_col = jnp.exp(b_col[last, :] - b_col + ig_col - m_last)
            decay = jnp.exp(a_col[last, :] - m_last)
            m_new = jnp.where(col[0:1, :] == h, m_last, m_new)
            per_head.append((inter, sv, dnm, w_col, decay))
        (i0, sv0, d0, w0, dc0), (i1, sv1, d1, w1, dc1) = per_head
        c_old = cbd[p]
        qc = _dot_nt(q_p.astype(BF16), c_old.astype(BF16))
        num = jnp.where(lane_lo, i0, i1) * qc + jnp.where(lane_lo, sv0, sv1)
        hv = (num / jnp.where(lane_lo, d0, d1))[0:rows, :]
        hv = hv * jax.nn.sigmoid(o_ref[:, sl])
        sq = hv * hv
        lo_r = lane_lo[0:rows, :]
        ms0 = jnp.sum(jnp.where(lo_r, sq, 0.0), axis=1, keepdims=True) * (1.0 / HEAD_DIM)
        ms1 = jnp.sum(jnp.where(lo_r, 0.0, sq), axis=1, keepdims=True) * (1.0 / HEAD_DIM)
        rs = jnp.where(lo_r, lax.rsqrt(ms0 + EPS), lax.rsqrt(ms1 + EPS))
        hm_ref[:, sl] = hv * rs * mhg_ref[:, sl]

        w_p = jnp.where(lane_lo, w0, w1)
        dec_p = jnp.where(lane_lo_row, dc0, dc1)
        c_upd = jnp.dot((v_p * w_p).T.astype(BF16), k_bf, preferred_element_type=F32)
        cbd[p] = dec_p * c_old + jnp.where(same_head, c_upd, 0.0)
        nst[:, sl] = dec_p * n_p + jnp.sum(k_p * w_p, axis=0, keepdims=True)
    mst[...] = m_new

    if nchunks > 1:
        ubuf[0:SUBLANE, :] = ubuf[CHUNK:CHUNK + SUBLANE, :]

    @pl.when(c == nchunks - 1)
    def _fin():
        cout_ref[0] = cbd[...]
        nout_ref[0] = nst[...]
        mout_ref[0] = mst[...]


def _mlstm(z, cprev, c0, n0, m0, cw, cb, bif, mhg, *, nseq, rows, nchunks):
    n = z.shape[0]
    tok = lambda b, c: b * nchunks + c
    const2 = lambda b, c: (0, 0)
    kern = functools.partial(_mlstm_kernel, rows=rows, nchunks=nchunks)
    npair = H_M // 2
    return pl.pallas_call(
        kern,
        grid=(nseq, nchunks),
        in_specs=[pl.BlockSpec((rows, 2 * W_M), lambda b, c: (tok(b, c), Z_QK // (2 * W_M))),
                  pl.BlockSpec((rows, W_M), lambda b, c: (tok(b, c), Z_V // W_M)),
                  pl.BlockSpec((rows, W_M), lambda b, c: (tok(b, c), Z_O // W_M)),
                  pl.BlockSpec((rows, LANE), lambda b, c: (tok(b, c), Z_G // LANE)),
                  pl.BlockSpec((1, CONV_W - 1, 2 * W_M), lambda b, c: (b, 0, 0)),
                  pl.BlockSpec((1, npair, LANE, LANE), lambda b, c: (b, 0, 0, 0)),
                  pl.BlockSpec((1, 1, W_M), lambda b, c: (b, 0, 0)),
                  pl.BlockSpec((1, 1, LANE), lambda b, c: (b, 0, 0)),
                  pl.BlockSpec((CONV_W, 2 * W_M), const2),
                  pl.BlockSpec((1, 2 * W_M), const2),
                  pl.BlockSpec((1, LANE), const2),
                  pl.BlockSpec((1, W_M), const2)],
        out_specs=[pl.BlockSpec((rows, W_M), lambda b, c: (tok(b, c), 0)),
                   pl.BlockSpec((1, npair, LANE, LANE), lambda b, c: (b, 0, 0, 0)),
                   pl.BlockSpec((1, 1, W_M), lambda b, c: (b, 0, 0)),
                   pl.BlockSpec((1, 1, LANE), lambda b, c: (b, 0, 0))],
        out_shape=[jax.ShapeDtypeStruct((n, W_M), F32),
                   jax.ShapeDtypeStruct((nseq, npair, LANE, LANE), F32),
                   jax.ShapeDtypeStruct((nseq, 1, W_M), F32),
                   jax.ShapeDtypeStruct((nseq, 1, LANE), F32)],
        scratch_shapes=[pltpu.VMEM((CHUNK + SUBLANE, 2 * W_M), F32),
                        pltpu.VMEM((npair, LANE, LANE), F32),
                        pltpu.VMEM((1, W_M), F32),
                        pltpu.VMEM((1, LANE), F32)],
        compiler_params=_params("parallel", "arbitrary"),
        name="mlstm",
    )(z, z, z, z, cprev, c0, n0, m0, cw, cb, bif, mhg)


def _to_blockdiag(c):
    lead = c.shape[:-3]
    cp = c.reshape(lead + (H_M // 2, 2, HEAD_DIM, HEAD_DIM))
    zero = jnp.zeros_like(cp[..., 0, :, :])
    top = jnp.concatenate([cp[..., 0, :, :], zero], axis=-1)
    bot = jnp.concatenate([zero, cp[..., 1, :, :]], axis=-1)
    return jnp.concatenate([top, bot], axis=-2)


def _from_blockdiag(cbd):
    lead = cbd.shape[:-3]
    c0 = cbd[..., :HEAD_DIM, :HEAD_DIM]
    c1 = cbd[..., HEAD_DIM:, HEAD_DIM:]
    return jnp.stack([c0, c1], axis=-3).reshape(lead + (H_M, HEAD_DIM, HEAD_DIM))


BLOCKS_PER_ITERATION = (4, 3, 2, 1)


def _unrolled_loop(count, body):
    if count == 0:
        return
    unroll = next(u for u in BLOCKS_PER_ITERATION if count % u == 0)
    if count == unroll:
        for i in range(count):
            body(i)
        return

    def step(it, carry):
        for u in range(unroll):
            body(it * unroll + u)
        return carry
    lax.fori_loop(0, count // unroll, step, 0)

def _attn_prompt_kernel(q_ref, k_ref, v_ref, tab_ref, out_ref, obuf, lbuf, *, seq):
    col = lax.broadcasted_iota(jnp.int32, (CHUNK, LANE), 1)
    lane_lo = col < HEAD_DIM

    def block(bi, dil, start, with_prev):
        def rows(st):
            return pl.ds(st, CHUNK) if dil == 1 else pl.ds(st, CHUNK, stride=dil)

        qb = q_ref[rows(start), :]
        kw = k_ref[rows(start), :]
        vw = v_ref[rows(start), :]
        if with_prev:
            prev = start - dil * CHUNK
            kw = jnp.concatenate([k_ref[rows(prev), :], kw], axis=0)
            vw = jnp.concatenate([v_ref[rows(prev), :], vw], axis=0)
        kw = kw.astype(BF16)
        vw = vw.astype(BF16)
        res = []
        for hh in range(2):
            sel = lane_lo if hh == 0 else jnp.logical_not(lane_lo)
            qm = jnp.where(sel, qb, 0.0).astype(BF16)
            bias = tab_ref[bi, hh] if with_prev else tab_ref[bi, hh, :, CHUNK:]
            lg = _dot_nt(qm, kw) + bias
            mx = jnp.max(lg, axis=1, keepdims=True)
            pr = jnp.exp(lg - mx)
            sm = jnp.sum(pr, axis=1, keepdims=True)
            pv = jnp.dot(pr.astype(BF16), vw, preferred_element_type=F32)
            res.append((pv / sm, mx + jnp.log(sm)))
        obuf[bi, rows(start), :] = jnp.where(lane_lo, res[0][0], res[1][0])
        lbuf[bi, rows(start), :] = jnp.where(lane_lo, res[0][1], res[1][1])

    for bi, (win, dil) in enumerate(DIL_BRANCHES):
        assert win // dil == CHUNK
        nblk = seq // (dil * CHUNK)

        def first(cls, bi=bi, dil=dil):
            block(bi, dil, cls, False)
        _unrolled_loop(dil, first)

        def rest(i, bi=bi, dil=dil):
            cls = i % dil
            blk = 1 + i // dil
            start = cls + dil * CHUNK * blk
            if dil == 1 and not isinstance(start, int):
                start = pl.multiple_of(start, CHUNK)
            block(bi, dil, start, True)
        _unrolled_loop(dil * (nblk - 1), rest)

    step = 2 * CHUNK

    def combine(i, carry):
        r = pl.ds(pl.multiple_of(i * step, step), step)
        l0, l1, l2 = lbuf[0, r, :], lbuf[1, r, :], lbuf[2, r, :]
        mx = jnp.maximum(jnp.maximum(l0, l1), l2)
        e0, e1, e2 = jnp.exp(l0 - mx), jnp.exp(l1 - mx), jnp.exp(l2 - mx)
        den = e0 + e1 + e2
        out_ref[r, :] = ((e0 / den) * obuf[0, r, :] + (e1 / den) * obuf[1, r, :]
                         + (e2 / den) * obuf[2, r, :])
        return carry
    lax.fori_loop(0, seq // step, combine, 0)


def _attn_prompt(z, tab, *, nseq, seq):
    n = z.shape[0]
    npair = H_A // 2
    nb = len(DIL_BRANCHES)
    kern = functools.partial(_attn_prompt_kernel, seq=seq)
    return pl.pallas_call(
        kern,
        grid=(nseq, npair),
        in_specs=[pl.BlockSpec((seq, LANE), lambda b, p: (b, Z_AQ // LANE + p)),
                  pl.BlockSpec((seq, LANE), lambda b, p: (b, Z_AK // LANE + p)),
                  pl.BlockSpec((seq, LANE), lambda b, p: (b, Z_AV // LANE + p)),
                  pl.BlockSpec((nb, 2, CHUNK, 2 * CHUNK), lambda b, p: (0, p, 0, 0))],
        out_specs=pl.BlockSpec((seq, LANE), lambda b, p: (b, p)),
        out_shape=jax.ShapeDtypeStruct((n, W_A), F32),
        scratch_shapes=[pltpu.VMEM((nb, seq, LANE), F32),
                        pltpu.VMEM((nb, seq, LANE), F32)],
        compiler_params=_params("parallel", "parallel"),
        name="attn_prompt",
    )(z, z, z, tab)


def _prompt_bias_table(rel_bias):
    qi = jnp.arange(CHUNK)[:, None]
    kj = jnp.arange(2 * CHUNK)[None, :]
    delta = qi + CHUNK - kj
    tabs = []
    for win, dil in DIL_BRANCHES:
        wc = win // dil
        valid = (delta >= 0) & (delta <= wc)
        bias = _bias_lookup(rel_bias, _rel_bucket(jnp.clip(delta, 0) * dil))
        tabs.append(jnp.where(valid[None], bias, NEG))
    return jnp.stack(tabs)


def _bias_lookup(rel_bias, bucket):
    onehot = (bucket[..., None] == jnp.arange(N_BUCKETS)).astype(F32)
    return jnp.einsum("...n,nh->h...", onehot, rel_bias.astype(F32),
                      precision=lax.Precision.HIGHEST)


def _bdot(a, b, contract):
    return lax.dot_general(a, b, ((contract[0], contract[1]), ((0,), (0,))),
                           preferred_element_type=F32)


def _attn_sample_kernel(li_ref, q_ref, kn_ref, vn_ref, kt_ref, vt_ref, *rest, tokens, wbuf):
    del li_ref
    nb = len(DIL_BRANCHES)
    tab_refs, tabn_ref, out_ref = rest[:nb], rest[nb], rest[nb + 1]
    q = q_ref[0].astype(BF16)
    kt = kt_ref[0, 0].astype(BF16)
    vt = vt_ref[0, 0].astype(BF16)
    pad = jnp.zeros((H_A, LANE - tokens, HEAD_DIM), F32)
    kn = jnp.concatenate([kn_ref[0], pad], axis=1).astype(BF16)
    vn = jnp.concatenate([vn_ref[0], pad], axis=1).astype(BF16)
    lg_cache = _bdot(q, kt, ((2,), (1,)))
    lg_new = _bdot(q, kn, ((2,), (2,)))

    probs, probs_new, stats = [], [], []
    for bi, (win, dil) in enumerate(DIL_BRANCHES):
        lo = wbuf - win
        lg = lg_cache[:, :, lo:] + tab_refs[bi][...]
        ln = lg_new + tabn_ref[bi]
        mx = jnp.maximum(jnp.max(lg, axis=-1, keepdims=True), jnp.max(ln, axis=-1, keepdims=True))
        pr = jnp.exp(lg - mx)
        pn = jnp.exp(ln - mx)
        sm = jnp.sum(pr, axis=-1, keepdims=True) + jnp.sum(pn, axis=-1, keepdims=True)
        if lo > 0:
            pr = jnp.concatenate([jnp.zeros((H_A, tokens, lo), F32), pr], axis=-1)
        probs.append(pr)
        probs_new.append(pn)
        stats.append((mx, sm))
    p_all = jnp.concatenate(probs, axis=1).astype(BF16)
    pn_all = jnp.concatenate(probs_new, axis=1).astype(BF16)
    o_all = _bdot(p_all, vt, ((2,), (2,))) + _bdot(pn_all, vn, ((2,), (1,)))

    lses = [mx + jnp.log(sm) for mx, sm in stats]
    top = functools.reduce(jnp.maximum, lses)
    es = [jnp.exp(l - top) for l in lses]
    den = functools.reduce(lambda a, b: a + b, es)
    out = None
    for bi in range(nb):
        term = (es[bi] / den) * (o_all[:, bi * tokens:(bi + 1) * tokens] / stats[bi][1])
        out = term if out is None else out + term
    out_ref[0] = out


def _attn_sample(layer, q, kn, vn, cache_kt, cache_vt, tabs, tabn):
    nseq, _, tokens, _ = q.shape
    wbuf = cache_kt.shape[-1]
    new_spec = pl.BlockSpec((1, H_A, tokens, HEAD_DIM), lambda b, li: (b, 0, 0, 0))
    cache_spec = pl.BlockSpec((1, 1, H_A, HEAD_DIM, wbuf), lambda b, li: (li[0], b, 0, 0, 0))
    tab_specs = [pl.BlockSpec(t.shape, lambda b, li: (0, 0, 0)) for t in tabs]
    kern = functools.partial(_attn_sample_kernel, tokens=tokens, wbuf=wbuf)
    return pl.pallas_call(
        kern,
        grid_spec=pltpu.PrefetchScalarGridSpec(
            num_scalar_prefetch=1,
            grid=(nseq,),
            in_specs=[new_spec, new_spec, new_spec, cache_spec, cache_spec, *tab_specs,
                      pl.BlockSpec(tabn.shape, lambda b, li: (0, 0, 0, 0))],
            out_specs=new_spec),
        out_shape=jax.ShapeDtypeStruct((nseq, H_A, tokens, HEAD_DIM), F32),
        compiler_params=_params("parallel"),
        name="attn_sample",
    )(layer, q, kn, vn, cache_kt, cache_vt, *tabs, tabn)


def _sample_bias_tables(rel_bias, tokens, wbuf):
    assert wbuf >= max(w for w, _ in DIL_BRANCHES)
    tok = jnp.arange(tokens)[:, None]
    tabs = []
    for win, dil in DIL_BRANCHES:
        pos = jnp.arange(wbuf - win, wbuf)[None, :]
        dist = wbuf + tok - pos
        valid = (dist % dil == 0) & (dist <= win)
        tabs.append(jnp.where(valid[None], _bias_lookup(rel_bias, _rel_bucket(dist)), NEG))
    other = jnp.arange(LANE)[None, :]
    dist = tok - other
    news = []
    for win, dil in DIL_BRANCHES:
        valid = (dist >= 0) & (dist % dil == 0) & (dist <= win)
        news.append(jnp.where(valid[None], _bias_lookup(rel_bias, _rel_bucket(jnp.clip(dist, 0))), NEG))
    return tabs, jnp.stack(news)


def _out_proj_kernel(x_ref, hm_ref, ha_ref, wo_ref, g2_ref, x1_ref, h2_ref):
    x1 = (x_ref[...]
          + jnp.dot(hm_ref[...].astype(BF16), wo_ref[0:W_M, :], preferred_element_type=F32)
          + jnp.dot(ha_ref[...].astype(BF16), wo_ref[W_M:, :], preferred_element_type=F32))
    x1_ref[...] = x1
    ms = jnp.mean(x1 * x1, axis=-1, keepdims=True)
    h2_ref[...] = (x1 * lax.rsqrt(ms + EPS) * g2_ref[...]).astype(BF16)


def _out_proj(x, hm, ha, wo, g2):
    n = x.shape[0]
    tm = min(ROWS_POST, n)
    const = lambda i: (0, 0)
    row = lambda i: (i, 0)
    return pl.pallas_call(
        _out_proj_kernel,
        grid=(n // tm,),
        in_specs=[pl.BlockSpec((tm, D_MODEL), row),
                  pl.BlockSpec((tm, W_M), row),
                  pl.BlockSpec((tm, W_A), row),
                  pl.BlockSpec((D_MODEL, D_MODEL), const),
                  pl.BlockSpec((1, D_MODEL), const)],
        out_specs=[pl.BlockSpec((tm, D_MODEL), row), pl.BlockSpec((tm, D_MODEL), row)],
        out_shape=[jax.ShapeDtypeStruct((n, D_MODEL), F32),
                   jax.ShapeDtypeStruct((n, D_MODEL), BF16)],
        compiler_params=_params("parallel"),
        name="out_proj",
    )(x, hm, ha, wo, g2)


def _mlp_ple_kernel(x1_ref, h2_ref, pe_ref, wg_ref, wu_ref, wd_ref, g3_ref, wpg_ref, wpe_ref,
                    out_ref, acc, *, nff):
    j = pl.program_id(1)

    @pl.when(j == 0)
    def _zero():
        acc[...] = jnp.zeros(acc.shape, F32)

    h2 = h2_ref[...]
    gate = jnp.dot(h2, wg_ref[...], preferred_element_type=F32)
    up = jnp.dot(h2, wu_ref[...], preferred_element_type=F32)
    act = (gate * jax.nn.sigmoid(gate) * up).astype(BF16)
    acc[...] += jnp.dot(act, wd_ref[...], preferred_element_type=F32)

    @pl.when(j == nff - 1)
    def _fin():
        x2 = x1_ref[...] + acc[...]
        ms = jnp.mean(x2 * x2, axis=-1, keepdims=True)
        h3 = (x2 * lax.rsqrt(ms + EPS) * g3_ref[...]).astype(BF16)
        pg = jax.nn.sigmoid(jnp.dot(h3, wpg_ref[...], preferred_element_type=F32))
        pp = jnp.dot(pe_ref[...].astype(BF16), wpe_ref[...], preferred_element_type=F32)
        out_ref[...] = x2 + pg * pp


def _mlp_ple(x1, h2, pe, wgu, wd, g3, wpg, wpe):
    n = x1.shape[0]
    tm = min(ROWS_POST, n)
    nff = D_FF // FF_BLOCK
    const = lambda i, j: (0, 0)
    row = lambda i, j: (i, 0)
    kern = functools.partial(_mlp_ple_kernel, nff=nff)
    return pl.pallas_call(
        kern,
        grid=(n // tm, nff),
        in_specs=[pl.BlockSpec((tm, D_MODEL), row),
                  pl.BlockSpec((tm, D_MODEL), row),
                  pl.BlockSpec((tm, D_PLE), row),
                  pl.BlockSpec((D_MODEL, FF_BLOCK), lambda i, j: (0, j)),
                  pl.BlockSpec((D_MODEL, FF_BLOCK), lambda i, j: (0, nff + j)),
                  pl.BlockSpec((FF_BLOCK, D_MODEL), lambda i, j: (j, 0)),
                  pl.BlockSpec((1, D_MODEL), const),
                  pl.BlockSpec((D_MODEL, D_MODEL), const),
                  pl.BlockSpec((D_PLE, D_MODEL), const)],
        out_specs=pl.BlockSpec((tm, D_MODEL), row),
        out_shape=jax.ShapeDtypeStruct((n, D_MODEL), F32),
        scratch_shapes=[pltpu.VMEM((tm, D_MODEL), F32)],
        compiler_params=_params("parallel", "arbitrary"),
        name="mlp_ple",
    )(x1, h2, pe, wgu, wgu, wd, g3, wpg, wpe)


def _mixer_tail(x, z, hm, ha, pe, lw):
    x1, h2 = _out_proj(x, hm, ha, lw["w_out"], lw["norm2_g"])
    return _mlp_ple(x1, h2, pe, lw["w_gu"], lw["w_down"], lw["ple_norm_g"], lw["w_pg"], lw["w_pe"])


def _layer(carry, lw, *, batch, seq, dec_batch, dec_seq, cache_k, cache_v, tab_p, tab_s, tab_n, seg):
    xp, xs = carry

    def proj(x):
        return _in_proj(x, lw["norm1_g"], lw["w_in"], lw["q_norm_g"], lw["k_norm_g"], seg)

    def mlstm(z, cprev, c0, n0, m0, nseq, rows, nchunks):
        return _mlstm(z, cprev, c0, n0, m0, lw["conv_w"], lw["conv_b"], lw["b_if"], lw["mh_norm_g"],
                      nseq=nseq, rows=rows, nchunks=nchunks)

    zp = proj(xp)
    hm, pc, pn, pm = mlstm(zp,
                           jnp.zeros((batch, CONV_W - 1, 2 * W_M), F32),
                           jnp.zeros((batch, H_M // 2, LANE, LANE), F32),
                           jnp.zeros((batch, 1, W_M), F32),
                           jnp.zeros((batch, 1, LANE), F32),
                           batch, CHUNK, seq // CHUNK)
    ha = _attn_prompt(zp, tab_p, nseq=batch, seq=seq)
    xp_new = _mixer_tail(xp, zp, hm, ha, lw["p_prompt"], lw)
    zp3 = zp.reshape(batch, seq, Z_W)
    keep = min(max(w for w, _ in DIL_BRANCHES), seq)
    p_out = (zp3[:, seq - keep:, Z_AK:Z_AK + W_A].reshape(batch, keep, H_A, HEAD_DIM),
             zp3[:, seq - keep:, Z_AV:Z_AV + W_A].reshape(batch, keep, H_A, HEAD_DIM),
             _from_blockdiag(pc),
             pn.reshape(batch, H_M, HEAD_DIM),
             pm[:, 0, :H_M],
             zp3[:, seq - (CONV_W - 1):, Z_QK:Z_QK + 2 * W_M])

    zs = proj(xs)
    hm, sc, sn, sm = mlstm(zs, lw["state_conv"], lw["state_C"], lw["state_n"], lw["state_m"],
                           dec_batch, dec_seq, 1)
    zs3 = zs.reshape(dec_batch, dec_seq, Z_W)
    heads = (dec_batch, dec_seq, H_A, HEAD_DIM)
    q_s = zs3[:, :, Z_AQ:Z_AQ + W_A].reshape(heads)
    k_s = zs3[:, :, Z_AK:Z_AK + W_A].reshape(heads)
    v_s = zs3[:, :, Z_AV:Z_AV + W_A].reshape(heads)
    by_head = lambda t: jnp.transpose(t, (0, 2, 1, 3))
    ha = _attn_sample(lw["layer"], by_head(q_s), by_head(k_s), by_head(v_s), cache_k, cache_v,
                      tab_s, tab_n)
    ha = by_head(ha).reshape(dec_batch * dec_seq, W_A)
    xs_new = _mixer_tail(xs, zs, hm, ha, lw["p_sample"], lw)
    conv_rows = jnp.concatenate([lw["state_conv"], zs3[:, :, Z_QK:Z_QK + 2 * W_M]], axis=1)
    s_out = (k_s, v_s,
             _from_blockdiag(sc),
             sn.reshape(dec_batch, H_M, HEAD_DIM),
             sm[:, 0, :H_M],
             conv_rows[:, -(CONV_W - 1):])
    return (xp_new, xs_new), (p_out, s_out)


def kernel(x_prompt, x_sample, p_prompt, p_sample, cache_attn_k, cache_attn_v, state_mlstm_C, state_mlstm_n, state_mlstm_m, state_conv, rel_bias, norm1_g, w_in, b_if, conv_w, conv_b, mh_norm_g, q_norm_g, k_norm_g, w_out, norm2_g, w_gu, w_down, ple_norm_g, w_pe, w_pg):
    batch, seq, _ = x_prompt.shape
    dec_batch, dec_seq, _ = x_sample.shape
    depth = w_in.shape[0]
    assert seq % (CHUNK * max(d for _, d in DIL_BRANCHES)) == 0

    n_m = 4 * W_M
    n_g = 2 * H_M
    w_in_z = jnp.concatenate(
        [w_in[:, :, :n_m], w_in[:, :, n_m + n_g:], w_in[:, :, n_m:n_m + n_g],
         jnp.zeros((depth, D_MODEL, LANE - n_g), w_in.dtype)], axis=-1).astype(BF16)

    head_of_lane = jnp.arange(W_A) // HEAD_DIM
    seg = (head_of_lane[:, None] == head_of_lane[None, :]).astype(BF16)
    tab_p = _prompt_bias_table(rel_bias)
    tab_s, tab_n = _sample_bias_tables(rel_bias, dec_seq, cache_attn_k.shape[2])
    cache_kt = jnp.transpose(cache_attn_k, (0, 1, 3, 4, 2))
    cache_vt = jnp.transpose(cache_attn_v, (0, 1, 3, 4, 2))

    row = lambda a: a[:, None, :]
    per_layer = dict(
        layer=jnp.arange(depth, dtype=jnp.int32)[:, None],
        norm1_g=row(norm1_g), w_in=w_in_z,
        q_norm_g=row(jnp.tile(q_norm_g, (1, H_A))), k_norm_g=row(jnp.tile(k_norm_g, (1, H_A))),
        conv_w=conv_w, conv_b=row(conv_b),
        b_if=row(jnp.pad(b_if, ((0, 0), (0, LANE - n_g)))),
        mh_norm_g=row(mh_norm_g),
        w_out=w_out.astype(BF16), norm2_g=row(norm2_g),
        w_gu=w_gu.astype(BF16), w_down=w_down.astype(BF16),
        ple_norm_g=row(ple_norm_g), w_pg=w_pg.astype(BF16), w_pe=w_pe.astype(BF16),
        p_prompt=p_prompt.reshape(depth, batch * seq, D_PLE),
        p_sample=p_sample.reshape(depth, dec_batch * dec_seq, D_PLE),
        state_conv=state_conv,
        state_C=_to_blockdiag(state_mlstm_C),
        state_n=state_mlstm_n.reshape(depth, dec_batch, 1, W_M),
        state_m=jnp.pad(state_mlstm_m, ((0, 0), (0, 0), (0, LANE - H_M)))[:, :, None, :],
    )
    step = functools.partial(_layer, batch=batch, seq=seq, dec_batch=dec_batch, dec_seq=dec_seq,
                             cache_k=cache_kt, cache_v=cache_vt,
                             tab_p=tab_p, tab_s=tab_s, tab_n=tab_n, seg=seg)
    carry = (x_prompt.reshape(batch * seq, D_MODEL), x_sample.reshape(dec_batch * dec_seq, D_MODEL))
    (xp, xs), (p_out, s_out) = lax.scan(step, carry, per_layer)
    return (xp.reshape(batch, seq, D_MODEL), xs.reshape(dec_batch, dec_seq, D_MODEL),
            *p_out, *s_out)
```

```python
import functools
import math

import jax
import jax.numpy as jnp
from jax import lax
from jax.experimental import pallas as pl
from jax.experimental.pallas import tpu as pltpu

F32 = jnp.float32
BF16 = jnp.bfloat16

D_MODEL = 1024
HEAD_DIM = 64
H_M = 8
H_A = 8
W_M = H_M * HEAD_DIM
W_A = H_A * HEAD_DIM
CONV_W = 4
CHUNK = 128
DIL_BRANCHES = ((128, 1), (512, 4), (2048, 16))
N_BUCKETS = 32
MAX_DIST = 2048
D_FF = 2816
D_PLE = 256
EPS = 1e-6
NEG = -1e30

LANE = 128
SUBLANE = 8
VMEM_LIMIT_BYTES = 56 * 1024 * 1024

Z_QK = 0
Z_V = 2 * W_M
Z_O = 3 * W_M
Z_AQ = 4 * W_M
Z_AK = Z_AQ + W_A
Z_AV = Z_AK + W_A
Z_G = Z_AV + W_A
Z_W = Z_G + LANE

ROWS_IN = 256
ROWS_POST = 512
FF_BLOCK = D_FF // 2


def _params(*sem):
    return pltpu.CompilerParams(dimension_semantics=sem, vmem_limit_bytes=VMEM_LIMIT_BYTES)


def _rel_bucket(dist):
    max_exact = N_BUCKETS // 2
    df = jnp.maximum(dist, 1).astype(F32)
    large = max_exact + (jnp.log(df / max_exact) / math.log(MAX_DIST / max_exact)
                         * (N_BUCKETS - max_exact)).astype(jnp.int32)
    large = jnp.minimum(large, N_BUCKETS - 1)
    return jnp.where(dist < max_exact, dist, large)


def _split_dot(p, ones_bf16):
    hi = p.astype(BF16)
    lo = (p - hi.astype(F32)).astype(BF16)
    return (jnp.dot(hi, ones_bf16, preferred_element_type=F32)
            + jnp.dot(lo, ones_bf16, preferred_element_type=F32))


def _dot_nt(a, b):
    return lax.dot_general(a, b, (((1,), (1,)), ((), ())), preferred_element_type=F32)


def _in_proj_kernel(x_ref, g_ref, w_ref, qg_ref, kg_ref, seg_ref, z_ref):
    x = x_ref[...]
    ms = jnp.mean(x * x, axis=-1, keepdims=True)
    h = (x * lax.rsqrt(ms + EPS) * g_ref[...]).astype(BF16)
    z_ref[:, 0:Z_AQ] = jnp.dot(h, w_ref[:, 0:Z_AQ], preferred_element_type=F32)
    for off, gr, mult in ((Z_AQ, qg_ref, HEAD_DIM ** -0.5), (Z_AK, kg_ref, None)):
        a = jnp.dot(h, w_ref[:, off:off + W_A], preferred_element_type=F32)
        ss = _split_dot(a * a, seg_ref[...])
        a = a * lax.rsqrt(ss * (1.0 / HEAD_DIM) + EPS) * gr[...]
        z_ref[:, off:off + W_A] = a if mult is None else a * mult
    z_ref[:, Z_AV:Z_W] = jnp.dot(h, w_ref[:, Z_AV:Z_W], preferred_element_type=F32)


def _in_proj(x, g, w, qg, kg, seg):
    n = x.shape[0]
    tm = min(ROWS_IN, n)
    const = lambda i: (0, 0)
    return pl.pallas_call(
        _in_proj_kernel,
        grid=(n // tm,),
        in_specs=[pl.BlockSpec((tm, D_MODEL), lambda i: (i, 0)),
                  pl.BlockSpec((1, D_MODEL), const),
                  pl.BlockSpec((D_MODEL, Z_W), const),
                  pl.BlockSpec((1, W_A), const),
                  pl.BlockSpec((1, W_A), const),
                  pl.BlockSpec((W_A, W_A), const)],
        out_specs=pl.BlockSpec((tm, Z_W), lambda i: (i, 0)),
        out_shape=jax.ShapeDtypeStruct((n, Z_W), F32),
        compiler_params=_params("parallel"),
        name="in_proj",
    )(x, g, w, qg, kg, seg)


def _log_sigmoid(x):
    return jnp.minimum(x, 0.0) - jnp.log1p(jnp.exp(-jnp.abs(x)))


def _cumsum_lanes(x):
    lane = lax.broadcasted_iota(jnp.int32, x.shape, 1)
    k = 1
    while k < x.shape[1]:
        x = x + jnp.where(lane >= k, pltpu.roll(x, k, axis=1), 0.0)
        k *= 2
    return x


def _mlstm_prompt_kernel(u_ref, v_ref, o_ref, g_ref, cw_ref, cb_ref, bif_ref, mhg_ref,
                         hm_ref, cout_ref, nout_ref, mout_ref,
                         ubuf, cbd, nst, mst, *, nchunks):
    c = pl.program_id(1)

    @pl.when(c == 0)
    def _init():
        ubuf[0:SUBLANE, :] = jnp.zeros((SUBLANE, 2 * W_M), F32)
        cbd[...] = jnp.zeros(cbd.shape, F32)
        nst[...] = jnp.zeros(nst.shape, F32)
        mst[...] = jnp.zeros(mst.shape, F32)

    ubuf[SUBLANE:SUBLANE + CHUNK, :] = u_ref[...]

    def conv_silu(lo):
        y = cb_ref[:, lo:lo + LANE]
        for j in range(CONV_W):
            off = SUBLANE - (CONV_W - 1) + j
            y = y + ubuf[off:off + CHUNK, lo:lo + LANE] * cw_ref[j:j + 1, lo:lo + LANE]
        return y * jax.nn.sigmoid(y)

    row = lax.broadcasted_iota(jnp.int32, (CHUNK, LANE), 0)
    col = lax.broadcasted_iota(jnp.int32, (CHUNK, LANE), 1)
    lane_lo = col < HEAD_DIM
    lane_lo_row = lane_lo[0:1, :]
    causal = row >= col
    same_head = (row < HEAD_DIM) == lane_lo

    ig_c = g_ref[...] + bif_ref[...]
    lf_c = _log_sigmoid(ig_c)
    ig_t = ig_c.T
    b_t = _cumsum_lanes(lf_c.T)
    b_c = b_t.T

    m_new = mst[...]
    last = slice(CHUNK - 1, CHUNK)
    for p in range(H_M // 2):
        sl = slice(p * LANE, (p + 1) * LANE)
        q_p = conv_silu(p * LANE)
        k_p = conv_silu(W_M + p * LANE) * HEAD_DIM ** -0.5
        k_bf = k_p.astype(BF16)
        v_p = v_ref[:, sl]
        v_bf = v_p.astype(BF16)
        n_p = nst[:, sl]
        per_head = []
        for hh in range(2):
            h = 2 * p + hh
            sel = lane_lo if hh == 0 else jnp.logical_not(lane_lo)
            ig_row = ig_t[h:h + 1, :]
            b_row = b_t[H_M + h:H_M + h + 1, :]
            b_col = b_c[:, H_M + h:H_M + h + 1]
            ig_col = ig_c[:, h:h + 1]
            a_col = b_col + mst[0:1, h:h + 1]
            dm = jnp.where(causal, b_col - b_row + ig_row, NEG)
            m_t = jnp.maximum(a_col, jnp.max(dm, axis=1, keepdims=True))
            inter = jnp.exp(a_col - m_t)
            qm = jnp.where(sel, q_p, 0.0).astype(BF16)
            s = _dot_nt(qm, k_bf) * jnp.exp(dm - m_t)
            ssum = jnp.sum(s, axis=1, keepdims=True)
            sv = jnp.dot(s.astype(BF16), v_bf, preferred_element_type=F32)
            qn = jnp.sum(jnp.where(sel, q_p * n_p, 0.0), axis=1, keepdims=True)
            den = inter * qn + ssum
            dnm = jnp.maximum(jnp.abs(den), jnp.exp(-m_t))
            m_last = m_t[last, :]
            w_col = jnp.exp(b_col[last, :] - b_col + ig_col - m_last)
            decay = jnp.exp(a_col[last, :] - m_last)
            m_new = jnp.where(col[0:1, :] == h, m_last, m_new)
            per_head.append((inter, sv, dnm, w_col, decay))
        (i0, sv0, d0, w0, dc0), (i1, sv1, d1, w1, dc1) = per_head
        c_old = cbd[p]
        qc = _dot_nt(q_p.astype(BF16), c_old.astype(BF16))
        num = jnp.where(lane_lo, i0, i1) * qc + jnp.where(lane_lo, sv0, sv1)
        hv = num / jnp.where(lane_lo, d0, d1)
        hv = hv * jax.nn.sigmoid(o_ref[:, sl])
        sq = hv * hv
        ms0 = jnp.sum(jnp.where(lane_lo, sq, 0.0), axis=1, keepdims=True) * (1.0 / HEAD_DIM)
        ms1 = jnp.sum(jnp.where(lane_lo, 0.0, sq), axis=1, keepdims=True) * (1.0 / HEAD_DIM)
        rs = jnp.where(lane_lo, lax.rsqrt(ms0 + EPS), lax.rsqrt(ms1 + EPS))
        hm_ref[:, sl] = hv * rs * mhg_ref[:, sl]

        w_p = jnp.where(lane_lo, w0, w1)
        dec_p = jnp.where(lane_lo_row, dc0, dc1)
        c_upd = jnp.dot((v_p * w_p).T.astype(BF16), k_bf, preferred_element_type=F32)
        cbd[p] = dec_p * c_old + jnp.where(same_head, c_upd, 0.0)
        nst[:, sl] = dec_p * n_p + jnp.sum(k_p * w_p, axis=0, keepdims=True)
    mst[...] = m_new
    ubuf[0:SUBLANE, :] = ubuf[CHUNK:CHUNK + SUBLANE, :]

    @pl.when(c == nchunks - 1)
    def _fin():
        cout_ref[0] = cbd[...]
        nout_ref[0] = nst[...]
        mout_ref[0] = mst[...]


def _mlstm_prompt(z, cw, cb, bif, mhg, *, nseq, nchunks):
    n = z.shape[0]
    tok = lambda b, c: b * nchunks + c
    const2 = lambda b, c: (0, 0)
    kern = functools.partial(_mlstm_prompt_kernel, nchunks=nchunks)
    npair = H_M // 2
    return pl.pallas_call(
        kern,
        grid=(nseq, nchunks),
        in_specs=[pl.BlockSpec((CHUNK, 2 * W_M), lambda b, c: (tok(b, c), Z_QK // (2 * W_M))),
                  pl.BlockSpec((CHUNK, W_M), lambda b, c: (tok(b, c), Z_V // W_M)),
                  pl.BlockSpec((CHUNK, W_M), lambda b, c: (tok(b, c), Z_O // W_M)),
                  pl.BlockSpec((CHUNK, LANE), lambda b, c: (tok(b, c), Z_G // LANE)),
                  pl.BlockSpec((CONV_W, 2 * W_M), const2),
                  pl.BlockSpec((1, 2 * W_M), const2),
                  pl.BlockSpec((1, LANE), const2),
                  pl.BlockSpec((1, W_M), const2)],
        out_specs=[pl.BlockSpec((CHUNK, W_M), lambda b, c: (tok(b, c), 0)),
                   pl.BlockSpec((1, npair, LANE, LANE), lambda b, c: (b, 0, 0, 0)),
                   pl.BlockSpec((1, 1, W_M), lambda b, c: (b, 0, 0)),
                   pl.BlockSpec((1, 1, LANE), lambda b, c: (b, 0, 0))],
        out_shape=[jax.ShapeDtypeStruct((n, W_M), F32),
                   jax.ShapeDtypeStruct((nseq, npair, LANE, LANE), F32),
                   jax.ShapeDtypeStruct((nseq, 1, W_M), F32),
                   jax.ShapeDtypeStruct((nseq, 1, LANE), F32)],
        scratch_shapes=[pltpu.VMEM((CHUNK + SUBLANE, 2 * W_M), F32),
                        pltpu.VMEM((npair, LANE, LANE), F32),
                        pltpu.VMEM((1, W_M), F32),
                        pltpu.VMEM((1, LANE), F32)],
        compiler_params=_params("parallel", "arbitrary"),
        name="mlstm_prompt",
    )(z, z, z, z, cw, cb, bif, mhg)


def _mlstm_sample_kernel(uq_ref, uk_ref, v_ref, o_ref, g_ref, pq_ref, pk_ref,
                         cwq_ref, cwk_ref, cbq_ref, cbk_ref, bif_ref, mhg_ref,
                         c_ref, n_ref, m_ref,
                         hm_ref, cout_ref, nout_ref, mout_ref,
                         q_s, k_s, vw_s, qc_s, *, tokens):
    def conv(u_ref, p_ref, cw_ref, cb_ref):
        rows = [p_ref[j] for j in range(CONV_W - 1)] + [u_ref[t] for t in range(tokens)]
        out = []
        for t in range(tokens):
            y = cb_ref[...]
            for j in range(CONV_W):
                y = y + rows[t + j] * cw_ref[j]
            out.append(y * jax.nn.sigmoid(y))
        return out

    for t, (qt, kt) in enumerate(zip(conv(uq_ref, pq_ref, cwq_ref, cbq_ref),
                                     conv(uk_ref, pk_ref, cwk_ref, cbk_ref))):
        q_s[t] = qt
        k_s[t] = kt * HEAD_DIM ** -0.5

    ig = [g_ref[0, t, 0:1, :] + bif_ref[0, 0:1, :] for t in range(tokens)]
    lf = [_log_sigmoid(g_ref[0, t, 1:2, :] + bif_ref[0, 1:2, :]) for t in range(tokens)]
    b = [lf[0]]
    for t in range(1, tokens):
        b.append(b[-1] + lf[t])
    m0 = m_ref[0]
    nvec = n_ref[0]
    a = [bt + m0 for bt in b]
    dmat = [[b[t] - b[s] + ig[s] for s in range(t + 1)] for t in range(tokens)]
    m_t = [functools.reduce(jnp.maximum, dmat[t], a[t]) for t in range(tokens)]
    inter = [jnp.exp(a[t] - m_t[t]) for t in range(tokens)]
    smat = [[jnp.sum(q_s[t] * k_s[s], axis=0, keepdims=True) * jnp.exp(dmat[t][s] - m_t[t])
             for s in range(t + 1)] for t in range(tokens)]
    den = [inter[t] * jnp.sum(q_s[t] * nvec, axis=0, keepdims=True)
           + functools.reduce(lambda x, y: x + y, smat[t]) for t in range(tokens)]
    dnm = [jnp.maximum(jnp.abs(den[t]), jnp.exp(-m_t[t])) for t in range(tokens)]
    m_last = m_t[-1]
    w = [jnp.exp(b[-1] - b[s] + ig[s] - m_last) for s in range(tokens)]
    decay = jnp.exp(a[-1] - m_last)
    for s in range(tokens):
        vw_s[s] = v_ref[s] * w[s]

    def per_vdim(f, carry):
        row = pl.ds(f, 1)
        cf = c_ref[0, f]
        for t in range(tokens):
            qc_s[t, row, :] = jnp.sum(q_s[t] * cf, axis=0, keepdims=True)
        upd = decay * cf
        for s in range(tokens):
            upd = upd + vw_s[s, row, :] * k_s[s]
        cout_ref[0, f] = upd
        return carry
    lax.fori_loop(0, HEAD_DIM, per_vdim, 0)

    n_new = decay * nvec
    for s in range(tokens):
        n_new = n_new + w[s] * k_s[s]
    nout_ref[0] = n_new
    mout_ref[0] = m_last
    for t in range(tokens):
        num = inter[t] * qc_s[t]
        for s in range(t + 1):
            num = num + smat[t][s] * v_ref[s]
        hv = num / dnm[t] * jax.nn.sigmoid(o_ref[t])
        ms = jnp.mean(hv * hv, axis=0, keepdims=True)
        hm_ref[t] = hv * lax.rsqrt(ms + EPS) * mhg_ref[...]


def _mlstm_sample(u_t, g_t, cprev_t, cw_b, cb_b, bif_b, mhg_b, c_t, n_t, m_t):
    tokens, _, nseq = u_t.shape
    e = HEAD_DIM
    tok_blk = lambda sec: pl.BlockSpec((tokens, e, nseq), lambda h, sec=sec: (0, sec * H_M + h, 0))
    prev_blk = lambda sec: pl.BlockSpec((CONV_W - 1, e, nseq), lambda h, sec=sec: (0, sec * H_M + h, 0))
    cw_blk = lambda sec: pl.BlockSpec((CONV_W, e, nseq), lambda h, sec=sec: (0, sec * H_M + h, 0))
    cb_blk = lambda sec: pl.BlockSpec((e, nseq), lambda h, sec=sec: (sec * H_M + h, 0))
    c_blk = pl.BlockSpec((1, e, e, nseq), lambda h: (h, 0, 0, 0))
    n_blk = pl.BlockSpec((1, e, nseq), lambda h: (h, 0, 0))
    m_blk = pl.BlockSpec((1, 1, nseq), lambda h: (h, 0, 0))
    kern = functools.partial(_mlstm_sample_kernel, tokens=tokens)
    return pl.pallas_call(
        kern,
        grid=(H_M,),
        in_specs=[tok_blk(0), tok_blk(1), tok_blk(2), tok_blk(3),
                  pl.BlockSpec((1, tokens, 2, nseq), lambda h: (h, 0, 0, 0)),
                  prev_blk(0), prev_blk(1), cw_blk(0), cw_blk(1), cb_blk(0), cb_blk(1),
                  pl.BlockSpec((1, 2, nseq), lambda h: (h, 0, 0)),
                  pl.BlockSpec((e, nseq), lambda h: (h, 0)),
                  c_blk, n_blk, m_blk],
        out_specs=[pl.BlockSpec((tokens, e, nseq), lambda h: (0, h, 0)), c_blk, n_blk, m_blk],
        out_shape=[jax.ShapeDtypeStruct((tokens, W_M, nseq), F32),
                   jax.ShapeDtypeStruct(c_t.shape, F32),
                   jax.ShapeDtypeStruct(n_t.shape, F32),
                   jax.ShapeDtypeStruct(m_t.shape, F32)],
        scratch_shapes=[pltpu.VMEM((tokens, e, nseq), F32) for _ in range(4)],
        compiler_params=_params("parallel"),
        name="mlstm_sample",
    )(u_t, u_t, u_t, u_t, g_t, cprev_t, cprev_t, cw_b, cw_b, cb_b, cb_b, bif_b, mhg_b, c_t, n_t, m_t)


def _from_blockdiag(cbd):
    lead = cbd.shape[:-3]
    c0 = cbd[..., :HEAD_DIM, :HEAD_DIM]
    c1 = cbd[..., HEAD_DIM:, HEAD_DIM:]
    return jnp.stack([c0, c1], axis=-3).reshape(lead + (H_M, HEAD_DIM, HEAD_DIM))


def _pipelined_loop(count, stages):
    depth = len(stages)

    def tick(t, static):
        for s in reversed(range(depth)):
            if not static or 0 <= t - s < count:
                stages[s](t - s)

    if count < depth:
        for t in range(count + depth - 1):
            tick(t, True)
        return
    for t in range(depth - 1):
        tick(t, True)

    def body(t, carry):
        tick(t, False)
        return carry
    lax.fori_loop(depth - 1, count, body, 0)
    for t in range(count, count + depth - 1):
        tick(t, True)

def _attn_prompt_kernel(q_ref, k_ref, v_ref, tab_ref, out_ref, obuf, lbuf, lg_s, p_s, s_s, l_s,
                        *, seq):
    col = lax.broadcasted_iota(jnp.int32, (CHUNK, LANE), 1)
    lane_lo = col < HEAD_DIM

    for bi, (win, dil) in enumerate(DIL_BRANCHES):
        assert win // dil == CHUNK
        nfirst = dil
        nrest = seq // CHUNK - dil
        assert 2 * (nfirst + nrest) <= lg_s.shape[0]

        def rows(st, dil=dil):
            return pl.ds(st, CHUNK) if dil == 1 else pl.ds(st, CHUNK, stride=dil)

        def place(i, with_prev, dil=dil, nfirst=nfirst):
            if not with_prev:
                return i, 2 * i, CHUNK
            start = i % dil + dil * CHUNK * (1 + i // dil)
            if dil == 1 and not isinstance(start, int):
                start = pl.multiple_of(start, CHUNK)
            return start, 2 * (nfirst + i), 2 * CHUNK

        def window(ref, start, with_prev, dil=dil, rows=rows):
            w = ref[rows(start), :]
            if with_prev:
                w = jnp.concatenate([ref[rows(start - dil * CHUNK), :], w], axis=0)
            return w.astype(BF16)

        def logits(i, with_prev, bi=bi, rows=rows, place=place, window=window):
            start, slot, width = place(i, with_prev)
            qb = q_ref[rows(start), :]
            kw = window(k_ref, start, with_prev)
            for hh in range(2):
                sel = lane_lo if hh == 0 else jnp.logical_not(lane_lo)
                qm = jnp.where(sel, qb, 0.0).astype(BF16)
                lg_s[slot + hh, :, 0:width] = (_dot_nt(qm, kw)
                                               + tab_ref[bi, hh, :, 2 * CHUNK - width:])

        def softmax(i, with_prev, place=place):
            _, slot, width = place(i, with_prev)
            for j in (slot, slot + 1):
                lg = lg_s[j, :, 0:width]
                mx = jnp.max(lg, axis=1, keepdims=True)
                pr = jnp.exp(lg - mx)
                sm = jnp.sum(pr, axis=1, keepdims=True)
                p_s[j, :, 0:width] = pr.astype(BF16)
                s_s[j] = jnp.broadcast_to(sm, (CHUNK, LANE))
                l_s[j] = jnp.broadcast_to(mx + jnp.log(sm), (CHUNK, LANE))

        def weighted(i, with_prev, bi=bi, rows=rows, place=place, window=window):
            start, slot, width = place(i, with_prev)
            vw = window(v_ref, start, with_prev)
            pv0 = jnp.dot(p_s[slot, :, 0:width], vw, preferred_element_type=F32)
            pv1 = jnp.dot(p_s[slot + 1, :, 0:width], vw, preferred_element_type=F32)
            obuf[bi, rows(start), :] = jnp.where(lane_lo, pv0 / s_s[slot], pv1 / s_s[slot + 1])
            lbuf[bi, rows(start), :] = jnp.where(lane_lo, l_s[slot], l_s[slot + 1])

        for with_prev, count in ((False, nfirst), (True, nrest)):
            _pipelined_loop(count, [functools.partial(stage, with_prev=with_prev)
                                    for stage in (logits, softmax, weighted)])

    step = 2 * CHUNK

    def combine(i, carry):
        r = pl.ds(pl.multiple_of(i * step, step), step)
        l0, l1, l2 = lbuf[0, r, :], lbuf[1, r, :], lbuf[2, r, :]
        mx = jnp.maximum(jnp.maximum(l0, l1), l2)
        e0, e1, e2 = jnp.exp(l0 - mx), jnp.exp(l1 - mx), jnp.exp(l2 - mx)
        den = e0 + e1 + e2
        out_ref[r, :] = ((e0 / den) * obuf[0, r, :] + (e1 / den) * obuf[1, r, :]
                         + (e2 / den) * obuf[2, r, :])
        return carry
    lax.fori_loop(0, seq // step, combine, 0)


def _attn_prompt(z, tab, *, nseq, seq):
    n = z.shape[0]
    npair = H_A // 2
    nb = len(DIL_BRANCHES)
    nslot = 2 * (seq // CHUNK)
    kern = functools.partial(_attn_prompt_kernel, seq=seq)
    return pl.pallas_call(
        kern,
        grid=(nseq, npair),
        in_specs=[pl.BlockSpec((seq, LANE), lambda b, p: (b, Z_AQ // LANE + p)),
                  pl.BlockSpec((seq, LANE), lambda b, p: (b, Z_AK // LANE + p)),
                  pl.BlockSpec((seq, LANE), lambda b, p: (b, Z_AV // LANE + p)),
                  pl.BlockSpec((nb, 2, CHUNK, 2 * CHUNK), lambda b, p: (0, p, 0, 0))],
        out_specs=pl.BlockSpec((seq, LANE), lambda b, p: (b, p)),
        out_shape=jax.ShapeDtypeStruct((n, W_A), F32),
        scratch_shapes=[pltpu.VMEM((nb, seq, LANE), F32),
                        pltpu.VMEM((nb, seq, LANE), F32),
                        pltpu.VMEM((nslot, CHUNK, 2 * CHUNK), F32),
                        pltpu.VMEM((nslot, CHUNK, 2 * CHUNK), BF16),
                        pltpu.VMEM((nslot, CHUNK, LANE), F32),
                        pltpu.VMEM((nslot, CHUNK, LANE), F32)],
        compiler_params=_params("parallel", "parallel"),
        name="attn_prompt",
    )(z, z, z, tab)


def _prompt_bias_table(rel_bias):
    qi = jnp.arange(CHUNK)[:, None]
    kj = jnp.arange(2 * CHUNK)[None, :]
    delta = qi + CHUNK - kj
    tabs = []
    for win, dil in DIL_BRANCHES:
        wc = win // dil
        valid = (delta >= 0) & (delta <= wc)
        bias = _bias_lookup(rel_bias, _rel_bucket(jnp.clip(delta, 0) * dil))
        tabs.append(jnp.where(valid[None], bias, NEG))
    return jnp.stack(tabs)


def _bias_lookup(rel_bias, bucket):
    onehot = (bucket[..., None] == jnp.arange(N_BUCKETS)).astype(F32)
    return jnp.einsum("...n,nh->h...", onehot, rel_bias.astype(F32),
                      precision=lax.Precision.HIGHEST)


def _bdot(a, b, contract):
    return lax.dot_general(a, b, ((contract[0], contract[1]), ((0,), (0,))),
                           preferred_element_type=F32)


def _attn_sample_kernel(li_ref, q_ref, kn_ref, vn_ref, kt_ref, vt_ref, *rest, tokens, wbuf):
    del li_ref
    nb = len(DIL_BRANCHES)
    tab_refs, tabn_ref, out_ref = rest[:nb], rest[nb], rest[nb + 1]
    q = q_ref[0].astype(BF16)
    kt = kt_ref[0, 0].astype(BF16)
    vt = vt_ref[0, 0].astype(BF16)
    pad = jnp.zeros((H_A, LANE - tokens, HEAD_DIM), F32)
    kn = jnp.concatenate([kn_ref[0], pad], axis=1).astype(BF16)
    vn = jnp.concatenate([vn_ref[0], pad], axis=1).astype(BF16)
    lg_cache = _bdot(q, kt, ((2,), (1,)))
    lg_new = _bdot(q, kn, ((2,), (2,)))

    probs, probs_new, stats = [], [], []
    for bi, (win, dil) in enumerate(DIL_BRANCHES):
        lo = wbuf - win
        lg = lg_cache[:, :, lo:] + tab_refs[bi][...]
        ln = lg_new + tabn_ref[bi]
        mx = jnp.maximum(jnp.max(lg, axis=-1, keepdims=True), jnp.max(ln, axis=-1, keepdims=True))
        pr = jnp.exp(lg - mx)
        pn = jnp.exp(ln - mx)
        sm = jnp.sum(pr, axis=-1, keepdims=True) + jnp.sum(pn, axis=-1, keepdims=True)
        if lo > 0:
            pr = jnp.concatenate([jnp.zeros((H_A, tokens, lo), F32), pr], axis=-1)
        probs.append(pr)
        probs_new.append(pn)
        stats.append((mx, sm))
    p_all = jnp.concatenate(probs, axis=1).astype(BF16)
    pn_all = jnp.concatenate(probs_new, axis=1).astype(BF16)
    o_all = _bdot(p_all, vt, ((2,), (2,))) + _bdot(pn_all, vn, ((2,), (1,)))

    lses = [mx + jnp.log(sm) for mx, sm in stats]
    top = functools.reduce(jnp.maximum, lses)
    es = [jnp.exp(l - top) for l in lses]
    den = functools.reduce(lambda a, b: a + b, es)
    out = None
    for bi in range(nb):
        term = (es[bi] / den) * (o_all[:, bi * tokens:(bi + 1) * tokens] / stats[bi][1])
        out = term if out is None else out + term
    out_ref[0] = out


def _attn_sample(layer, q, kn, vn, cache_kt, cache_vt, tabs, tabn):
    nseq, _, tokens, _ = q.shape
    wbuf = cache_kt.shape[-1]
    new_spec = pl.BlockSpec((1, H_A, tokens, HEAD_DIM), lambda b, li: (b, 0, 0, 0))
    cache_spec = pl.BlockSpec((1, 1, H_A, HEAD_DIM, wbuf), lambda b, li: (li[0], b, 0, 0, 0))
    tab_specs = [pl.BlockSpec(t.shape, lambda b, li: (0, 0, 0)) for t in tabs]
    kern = functools.partial(_attn_sample_kernel, tokens=tokens, wbuf=wbuf)
    return pl.pallas_call(
        kern,
        grid_spec=pltpu.PrefetchScalarGridSpec(
            num_scalar_prefetch=1,
            grid=(nseq,),
            in_specs=[new_spec, new_spec, new_spec, cache_spec, cache_spec, *tab_specs,
                      pl.BlockSpec(tabn.shape, lambda b, li: (0, 0, 0, 0))],
            out_specs=new_spec),
        out_shape=jax.ShapeDtypeStruct((nseq, H_A, tokens, HEAD_DIM), F32),
        compiler_params=_params("parallel"),
        name="attn_sample",
    )(layer, q, kn, vn, cache_kt, cache_vt, *tabs, tabn)


def _sample_bias_tables(rel_bias, tokens, wbuf):
    assert wbuf >= max(w for w, _ in DIL_BRANCHES)
    tok = jnp.arange(tokens)[:, None]
    tabs = []
    for win, dil in DIL_BRANCHES:
        pos = jnp.arange(wbuf - win, wbuf)[None, :]
        dist = wbuf + tok - pos
        valid = (dist % dil == 0) & (dist <= win)
        tabs.append(jnp.where(valid[None], _bias_lookup(rel_bias, _rel_bucket(dist)), NEG))
    other = jnp.arange(LANE)[None, :]
    dist = tok - other
    news = []
    for win, dil in DIL_BRANCHES:
        valid = (dist >= 0) & (dist % dil == 0) & (dist <= win)
        news.append(jnp.where(valid[None], _bias_lookup(rel_bias, _rel_bucket(jnp.clip(dist, 0))), NEG))
    return tabs, jnp.stack(news)


def _out_proj_kernel(x_ref, hm_ref, ha_ref, wo_ref, g2_ref, x1_ref, h2_ref):
    x1 = (x_ref[...]
          + jnp.dot(hm_ref[...].astype(BF16), wo_ref[0:W_M, :], preferred_element_type=F32)
          + jnp.dot(ha_ref[...].astype(BF16), wo_ref[W_M:, :], preferred_element_type=F32))
    x1_ref[...] = x1
    ms = jnp.mean(x1 * x1, axis=-1, keepdims=True)
    h2_ref[...] = (x1 * lax.rsqrt(ms + EPS) * g2_ref[...]).astype(BF16)


def _out_proj(x, hm, ha, wo, g2):
    n = x.shape[0]
    tm = min(ROWS_POST, n)
    const = lambda i: (0, 0)
    row = lambda i: (i, 0)
    return pl.pallas_call(
        _out_proj_kernel,
        grid=(n // tm,),
        in_specs=[pl.BlockSpec((tm, D_MODEL), row),
                  pl.BlockSpec((tm, W_M), row),
                  pl.BlockSpec((tm, W_A), row),
                  pl.BlockSpec((D_MODEL, D_MODEL), const),
                  pl.BlockSpec((1, D_MODEL), const)],
        out_specs=[pl.BlockSpec((tm, D_MODEL), row), pl.BlockSpec((tm, D_MODEL), row)],
        out_shape=[jax.ShapeDtypeStruct((n, D_MODEL), F32),
                   jax.ShapeDtypeStruct((n, D_MODEL), BF16)],
        compiler_params=_params("parallel"),
        name="out_proj",
    )(x, hm, ha, wo, g2)


def _mlp_ple_kernel(x1_ref, h2_ref, pe_ref, wg_ref, wu_ref, wd_ref, g3_ref, wpg_ref, wpe_ref,
                    out_ref, acc, *, nff):
    j = pl.program_id(1)

    @pl.when(j == 0)
    def _zero():
        acc[...] = jnp.zeros(acc.shape, F32)

    h2 = h2_ref[...]
    gate = jnp.dot(h2, wg_ref[...], preferred_element_type=F32)
    up = jnp.dot(h2, wu_ref[...], preferred_element_type=F32)
    act = (gate * jax.nn.sigmoid(gate) * up).astype(BF16)
    acc[...] += jnp.dot(act, wd_ref[...], preferred_element_type=F32)

    @pl.when(j == nff - 1)
    def _fin():
        x2 = x1_ref[...] + acc[...]
        ms = jnp.mean(x2 * x2, axis=-1, keepdims=True)
        h3 = (x2 * lax.rsqrt(ms + EPS) * g3_ref[...]).astype(BF16)
        pg = jax.nn.sigmoid(jnp.dot(h3, wpg_ref[...], preferred_element_type=F32))
        pp = jnp.dot(pe_ref[...].astype(BF16), wpe_ref[...], preferred_element_type=F32)
        out_ref[...] = x2 + pg * pp


def _mlp_ple(x1, h2, pe, wgu, wd, g3, wpg, wpe):
    n = x1.shape[0]
    tm = min(ROWS_POST, n)
    nff = D_FF // FF_BLOCK
    const = lambda i, j: (0, 0)
    row = lambda i, j: (i, 0)
    kern = functools.partial(_mlp_ple_kernel, nff=nff)
    return pl.pallas_call(
        kern,
        grid=(n // tm, nff),
        in_specs=[pl.BlockSpec((tm, D_MODEL), row),
                  pl.BlockSpec((tm, D_MODEL), row),
                  pl.BlockSpec((tm, D_PLE), row),
                  pl.BlockSpec((D_MODEL, FF_BLOCK), lambda i, j: (0, j)),
                  pl.BlockSpec((D_MODEL, FF_BLOCK), lambda i, j: (0, nff + j)),
                  pl.BlockSpec((FF_BLOCK, D_MODEL), lambda i, j: (j, 0)),
                  pl.BlockSpec((1, D_MODEL), const),
                  pl.BlockSpec((D_MODEL, D_MODEL), const),
                  pl.BlockSpec((D_PLE, D_MODEL), const)],
        out_specs=pl.BlockSpec((tm, D_MODEL), row),
        out_shape=jax.ShapeDtypeStruct((n, D_MODEL), F32),
        scratch_shapes=[pltpu.VMEM((tm, D_MODEL), F32)],
        compiler_params=_params("parallel", "arbitrary"),
        name="mlp_ple",
    )(x1, h2, pe, wgu, wgu, wd, g3, wpg, wpe)


def _mixer_tail(x, z, hm, ha, pe, lw):
    x1, h2 = _out_proj(x, hm, ha, lw["w_out"], lw["norm2_g"])
    return _mlp_ple(x1, h2, pe, lw["w_gu"], lw["w_down"], lw["ple_norm_g"], lw["w_pg"], lw["w_pe"])


def _layer(xp, xs, lw, *, batch, seq, dec_batch, dec_seq, cache_k, cache_v, tab_p, tab_s, tab_n, seg):
    def proj(x):
        return _in_proj(x, lw["norm1_g"], lw["w_in"], lw["q_norm_g"], lw["k_norm_g"], seg)

    zp = proj(xp)
    hm, pc, pn, pm = _mlstm_prompt(zp, lw["conv_w"], lw["conv_b"], lw["b_if"], lw["mh_norm_g"],
                                   nseq=batch, nchunks=seq // CHUNK)
    ha = _attn_prompt(zp, tab_p, nseq=batch, seq=seq)
    xp_new = _mixer_tail(xp, zp, hm, ha, lw["p_prompt"], lw)
    zp3 = zp.reshape(batch, seq, Z_W)
    keep = min(max(w for w, _ in DIL_BRANCHES), seq)
    p_out = (zp3[:, seq - keep:, Z_AK:Z_AK + W_A].reshape(batch, keep, H_A, HEAD_DIM),
             zp3[:, seq - keep:, Z_AV:Z_AV + W_A].reshape(batch, keep, H_A, HEAD_DIM),
             _from_blockdiag(pc),
             pn.reshape(batch, H_M, HEAD_DIM),
             pm[:, 0, :H_M],
             zp3[:, seq - (CONV_W - 1):, Z_QK:Z_QK + 2 * W_M])

    zs = proj(xs)
    zs3 = zs.reshape(dec_batch, dec_seq, Z_W)
    lanes = lambda a: jnp.broadcast_to(a[..., None], a.shape + (dec_batch,))
    gates = zs3[:, :, Z_G:Z_G + 2 * H_M].reshape(dec_batch, dec_seq, 2, H_M)
    hm_t, sc, sn, sm = _mlstm_sample(
        jnp.transpose(zs3[:, :, :Z_AQ], (1, 2, 0)),
        jnp.transpose(gates, (3, 1, 2, 0)),
        jnp.transpose(lw["state_conv"], (1, 2, 0)),
        lanes(lw["conv_w"]), lanes(lw["conv_b"][0]),
        lanes(lw["b_if"][0, :2 * H_M].reshape(2, H_M).T), lanes(lw["mh_norm_g"][0]),
        jnp.transpose(lw["state_C"], (1, 2, 3, 0)),
        jnp.transpose(lw["state_n"], (1, 2, 0)),
        jnp.transpose(lw["state_m"], (1, 0))[:, None, :])
    hm = jnp.transpose(hm_t, (2, 0, 1)).reshape(dec_batch * dec_seq, W_M)
    heads = (dec_batch, dec_seq, H_A, HEAD_DIM)
    q_s = zs3[:, :, Z_AQ:Z_AQ + W_A].reshape(heads)
    k_s = zs3[:, :, Z_AK:Z_AK + W_A].reshape(heads)
    v_s = zs3[:, :, Z_AV:Z_AV + W_A].reshape(heads)
    by_head = lambda t: jnp.transpose(t, (0, 2, 1, 3))
    ha = _attn_sample(lw["layer"], by_head(q_s), by_head(k_s), by_head(v_s), cache_k, cache_v,
                      tab_s, tab_n)
    ha = by_head(ha).reshape(dec_batch * dec_seq, W_A)
    xs_new = _mixer_tail(xs, zs, hm, ha, lw["p_sample"], lw)
    conv_rows = jnp.concatenate([lw["state_conv"], zs3[:, :, Z_QK:Z_QK + 2 * W_M]], axis=1)
    s_out = (k_s, v_s,
             jnp.transpose(sc, (3, 0, 1, 2)),
             jnp.transpose(sn, (2, 0, 1)),
             jnp.transpose(sm[:, 0, :], (1, 0)),
             conv_rows[:, -(CONV_W - 1):])
    return xp_new, xs_new, p_out, s_out


def kernel(x_prompt, x_sample, p_prompt, p_sample, cache_attn_k, cache_attn_v, state_mlstm_C, state_mlstm_n, state_mlstm_m, state_conv, rel_bias, norm1_g, w_in, b_if, conv_w, conv_b, mh_norm_g, q_norm_g, k_norm_g, w_out, norm2_g, w_gu, w_down, ple_norm_g, w_pe, w_pg):
    batch, seq, _ = x_prompt.shape
    dec_batch, dec_seq, _ = x_sample.shape
    depth = w_in.shape[0]
    assert seq % (CHUNK * max(d for _, d in DIL_BRANCHES)) == 0

    n_m = 4 * W_M
    n_g = 2 * H_M
    w_in_z = jnp.concatenate(
        [w_in[:, :, :n_m], w_in[:, :, n_m + n_g:], w_in[:, :, n_m:n_m + n_g],
         jnp.zeros((depth, D_MODEL, LANE - n_g), w_in.dtype)], axis=-1).astype(BF16)

    head_of_lane = jnp.arange(W_A) // HEAD_DIM
    seg = (head_of_lane[:, None] == head_of_lane[None, :]).astype(BF16)
    tab_p = _prompt_bias_table(rel_bias)
    tab_s, tab_n = _sample_bias_tables(rel_bias, dec_seq, cache_attn_k.shape[2])
    cache_kt = jnp.transpose(cache_attn_k, (0, 1, 3, 4, 2))
    cache_vt = jnp.transpose(cache_attn_v, (0, 1, 3, 4, 2))

    row = lambda a: a[:, None, :]
    per_layer = dict(
        layer=jnp.arange(depth, dtype=jnp.int32)[:, None],
        norm1_g=row(norm1_g), w_in=w_in_z,
        q_norm_g=row(jnp.tile(q_norm_g, (1, H_A))), k_norm_g=row(jnp.tile(k_norm_g, (1, H_A))),
        conv_w=conv_w, conv_b=row(conv_b),
        b_if=row(jnp.pad(b_if, ((0, 0), (0, LANE - n_g)))),
        mh_norm_g=row(mh_norm_g),
        w_out=w_out.astype(BF16), norm2_g=row(norm2_g),
        w_gu=w_gu.astype(BF16), w_down=w_down.astype(BF16),
        ple_norm_g=row(ple_norm_g), w_pg=w_pg.astype(BF16), w_pe=w_pe.astype(BF16),
        p_prompt=p_prompt.reshape(depth, batch * seq, D_PLE),
        p_sample=p_sample.reshape(depth, dec_batch * dec_seq, D_PLE),
        state_conv=state_conv, state_C=state_mlstm_C, state_n=state_mlstm_n, state_m=state_mlstm_m,
    )
    xp = x_prompt.reshape(batch * seq, D_MODEL)
    xs = x_sample.reshape(dec_batch * dec_seq, D_MODEL)
    p_outs, s_outs = [], []
    for i in range(depth):
        lw = {name: val[i] for name, val in per_layer.items()}
        xp, xs, p_out, s_out = _layer(xp, xs, lw, batch=batch, seq=seq, dec_batch=dec_batch,
                                      dec_seq=dec_seq, cache_k=cache_kt, cache_v=cache_vt,
                                      tab_p=tab_p, tab_s=tab_s, tab_n=tab_n, seg=seg)
        p_outs.append(p_out)
        s_outs.append(s_out)
    stack = lambda outs: tuple(jnp.stack(leaf) for leaf in zip(*outs))
    return (xp.reshape(batch, seq, D_MODEL), xs.reshape(dec_batch, dec_seq, D_MODEL),
            *stack(p_outs), *stack(s_outs))
```

```python
import functools
import math

import jax
import jax.numpy as jnp
import numpy as np
from jax import lax
from jax.experimental import pallas as pl
from jax.experimental.pallas import tpu as pltpu

F32 = jnp.float32
BF16 = jnp.bfloat16

D_MODEL = 1024
HEAD_DIM = 64
H_M = 8
H_A = 8
W_M = H_M * HEAD_DIM
W_A = H_A * HEAD_DIM
CONV_W = 4
CHUNK = 128
DIL_BRANCHES = ((128, 1), (512, 4), (2048, 16))
N_BUCKETS = 32
MAX_DIST = 2048
D_FF = 2816
D_PLE = 256
EPS = 1e-6
NEG = -1e30

LANE = 128
SUBLANE = 8
VMEM_LIMIT_BYTES = 56 * 1024 * 1024

Z_QK = 0
Z_V = 2 * W_M
Z_O = 3 * W_M
Z_AQ = 4 * W_M
Z_AK = Z_AQ + W_A
Z_AV = Z_AK + W_A
Z_GI = Z_AV + W_A
Z_GF = Z_GI + LANE
Z_W = Z_GF + LANE

ROWS_IN = 256
ROWS_POST = 512
FF_BLOCK = D_FF // 2


def _params(*sem):
    return pltpu.CompilerParams(dimension_semantics=sem, vmem_limit_bytes=VMEM_LIMIT_BYTES)


def _rel_bucket(dist):
    max_exact = N_BUCKETS // 2
    df = jnp.maximum(dist, 1).astype(F32)
    large = max_exact + (jnp.log(df / max_exact) / math.log(MAX_DIST / max_exact)
                         * (N_BUCKETS - max_exact)).astype(jnp.int32)
    large = jnp.minimum(large, N_BUCKETS - 1)
    return jnp.where(dist < max_exact, dist, large)


def _split_dot(p, ones_bf16):
    hi = p.astype(BF16)
    lo = (p - hi.astype(F32)).astype(BF16)
    return (jnp.dot(hi, ones_bf16, preferred_element_type=F32)
            + jnp.dot(lo, ones_bf16, preferred_element_type=F32))


def _dot_nt(a, b):
    return lax.dot_general(a, b, (((1,), (1,)), ((), ())), preferred_element_type=F32)


def _in_proj_kernel(x_ref, g_ref, w_ref, qg_ref, kg_ref, seg_ref, z_ref):
    x = x_ref[...]
    ms = jnp.mean(x * x, axis=-1, keepdims=True)
    h = (x * lax.rsqrt(ms + EPS) * g_ref[...]).astype(BF16)
    z_ref[:, 0:Z_AQ] = jnp.dot(h, w_ref[:, 0:Z_AQ], preferred_element_type=F32)
    for off, gr, mult in ((Z_AQ, qg_ref, HEAD_DIM ** -0.5), (Z_AK, kg_ref, None)):
        a = jnp.dot(h, w_ref[:, off:off + W_A], preferred_element_type=F32)
        ss = _split_dot(a * a, seg_ref[...])
        a = a * lax.rsqrt(ss * (1.0 / HEAD_DIM) + EPS) * gr[...]
        z_ref[:, off:off + W_A] = a if mult is None else a * mult
    z_ref[:, Z_AV:Z_W] = jnp.dot(h, w_ref[:, Z_AV:Z_W], preferred_element_type=F32)


def _in_proj(x, g, w, qg, kg, seg):
    n = x.shape[0]
    tm = min(ROWS_IN, n)
    const = lambda i: (0, 0)
    return pl.pallas_call(
        _in_proj_kernel,
        grid=(n // tm,),
        in_specs=[pl.BlockSpec((tm, D_MODEL), lambda i: (i, 0)),
                  pl.BlockSpec((1, D_MODEL), const),
                  pl.BlockSpec((D_MODEL, Z_W), const),
                  pl.BlockSpec((1, W_A), const),
                  pl.BlockSpec((1, W_A), const),
                  pl.BlockSpec((W_A, W_A), const)],
        out_specs=pl.BlockSpec((tm, Z_W), lambda i: (i, 0)),
        out_shape=jax.ShapeDtypeStruct((n, Z_W), F32),
        compiler_params=_params("parallel"),
        name="in_proj",
    )(x, g, w, qg, kg, seg)


def _log_sigmoid(x):
    return jnp.minimum(x, 0.0) - jnp.log1p(jnp.exp(-jnp.abs(x)))


def _scan_rows(x, combine, identity):
    row = lax.broadcasted_iota(jnp.int32, x.shape, 0)
    k = 1
    while k < x.shape[0]:
        x = combine(x, jnp.where(row >= k, pltpu.roll(x, k, axis=0), identity))
        k *= 2
    return x


def _bf16_parts(x, count):
    parts = []
    for _ in range(count):
        part = x.astype(BF16)
        parts.append(part)
        x = x - part.astype(F32)
    return parts


def _expand(parts, ones_bf16):
    out = None
    for part in parts:
        term = jnp.dot(part, ones_bf16, preferred_element_type=F32)
        out = term if out is None else out + term
    return out


def _mlstm_prompt_kernel(u_ref, v_ref, o_ref, gi_ref, gf_ref, cw_ref, cb_ref, bi_ref, bf_ref, mhg_ref,
                         exh_ref, exp_ref, ones_ref, seg_ref,
                         hm_ref, cout_ref, nout_ref, mout_ref,
                         ubuf, cbd, nst, mst, *, nchunks):
    c = pl.program_id(1)

    @pl.when(c == 0)
    def _init():
        ubuf[0:SUBLANE, :] = jnp.zeros((SUBLANE, 2 * W_M), F32)
        cbd[...] = jnp.zeros(cbd.shape, F32)
        nst[...] = jnp.zeros(nst.shape, F32)
        mst[...] = jnp.zeros(mst.shape, F32)

    ubuf[SUBLANE:SUBLANE + CHUNK, :] = u_ref[...]

    def conv_silu(lo):
        y = cb_ref[:, lo:lo + LANE]
        for j in range(CONV_W):
            off = SUBLANE - (CONV_W - 1) + j
            y = y + ubuf[off:off + CHUNK, lo:lo + LANE] * cw_ref[j:j + 1, lo:lo + LANE]
        return y * jax.nn.sigmoid(y)

    row = lax.broadcasted_iota(jnp.int32, (CHUNK, LANE), 0)
    col = lax.broadcasted_iota(jnp.int32, (CHUNK, LANE), 1)
    lane_lo = col < HEAD_DIM
    causal = row >= col
    same_head = (row < HEAD_DIM) == lane_lo
    last = slice(CHUNK - 1, CHUNK)

    b = _scan_rows(_log_sigmoid(gf_ref[...] + bf_ref[...]), jnp.add, 0.0)
    g = gi_ref[...] + bi_ref[...] - b
    m_prev = mst[...]
    big_m = jnp.maximum(m_prev, _scan_rows(g, jnp.maximum, -jnp.inf))
    m_t = b + big_m
    mst[...] = m_t[last, :]
    g_t = g.T
    big_m_parts = _bf16_parts(big_m, 3)
    inter_parts = _bf16_parts(jnp.exp(m_prev - big_m), 2)
    floor_parts = _bf16_parts(jnp.exp(-m_t), 2)
    w_end_parts = _bf16_parts(jnp.exp(g - big_m[last, :]), 2)

    for p in range(H_M // 2):
        sl = slice(p * LANE, (p + 1) * LANE)
        big_m_2 = _expand(big_m_parts, exh_ref[p])
        big_m_h = (big_m_2[:, 0:LANE], big_m_2[:, LANE:])
        inter_p = _expand(inter_parts, exp_ref[p])
        floor_p = _expand(floor_parts, exp_ref[p])
        w_p = _expand(w_end_parts, exp_ref[p])

        q_p = conv_silu(p * LANE)
        k_p = conv_silu(W_M + p * LANE) * HEAD_DIM ** -0.5
        q_bf = q_p.astype(BF16)
        k_bf = k_p.astype(BF16)
        v_p = v_ref[:, sl]
        n_p = nst[:, sl]
        c_old = cbd[p]

        s_heads = []
        for hh in range(2):
            h = 2 * p + hh
            sel = lane_lo if hh == 0 else jnp.logical_not(lane_lo)
            qm = jnp.where(sel, q_p, 0.0).astype(BF16)
            decay_mat = jnp.exp(jnp.where(causal, g_t[h:h + 1, :] - big_m_h[hh], NEG))
            s_heads.append((_dot_nt(qm, k_bf) * decay_mat).astype(BF16))
        s_cat = jnp.concatenate(s_heads, axis=1)
        v_cat = jnp.concatenate([jnp.where(lane_lo, v_p, 0.0), jnp.where(lane_lo, 0.0, v_p)], axis=0)
        intra = jnp.dot(s_cat, jnp.concatenate([v_cat.astype(BF16), ones_ref[...]], axis=1),
                        preferred_element_type=F32)
        n_mat = jnp.where(same_head, jnp.broadcast_to(n_p, (LANE, LANE)), 0.0)
        carried = _dot_nt(q_bf, jnp.concatenate([c_old, n_mat], axis=0).astype(BF16))
        num = inter_p * carried[:, 0:LANE] + intra[:, 0:LANE]
        den = inter_p * carried[:, LANE:] + intra[:, LANE:]
        hv = num / jnp.maximum(jnp.abs(den), floor_p)
        hv = hv * jax.nn.sigmoid(o_ref[:, sl])
        ms = _split_dot(hv * hv, seg_ref[...]) * (1.0 / HEAD_DIM)
        hm_ref[:, sl] = hv * lax.rsqrt(ms + EPS) * mhg_ref[:, sl]

        dec_p = inter_p[last, :]
        c_upd = jnp.dot((v_p * w_p).T.astype(BF16), k_bf, preferred_element_type=F32)
        cbd[p] = dec_p * c_old + jnp.where(same_head, c_upd, 0.0)
        nst[:, sl] = dec_p * n_p + jnp.sum(k_p * w_p, axis=0, keepdims=True)
    ubuf[0:SUBLANE, :] = ubuf[CHUNK:CHUNK + SUBLANE, :]

    @pl.when(c == nchunks - 1)
    def _fin():
        cout_ref[0] = cbd[...]
        nout_ref[0] = nst[...]
        mout_ref[0] = mst[...]


def _mlstm_constants():
    npair = H_M // 2
    half = np.arange(LANE) // HEAD_DIM
    per_head = np.zeros((npair, LANE, 2 * LANE), np.float32)
    per_pair = np.zeros((npair, LANE, LANE), np.float32)
    for p in range(npair):
        per_head[p, 2 * p, 0:LANE] = 1.0
        per_head[p, 2 * p + 1, LANE:] = 1.0
        per_pair[p, 2 * p + half, np.arange(LANE)] = 1.0
    ones = np.zeros((2 * CHUNK, LANE), np.float32)
    ones[:CHUNK, :HEAD_DIM] = 1.0
    ones[CHUNK:, HEAD_DIM:] = 1.0
    seg = (half[:, None] == half[None, :]).astype(np.float32)
    return tuple(jnp.asarray(a, BF16) for a in (per_head, per_pair, ones, seg))


def _mlstm_prompt(z, cw, cb, b_i, b_f, mhg, *, nseq, nchunks):
    n = z.shape[0]
    tok = lambda b, c: b * nchunks + c
    const2 = lambda b, c: (0, 0)
    const3 = lambda b, c: (0, 0, 0)
    kern = functools.partial(_mlstm_prompt_kernel, nchunks=nchunks)
    npair = H_M // 2
    per_head, per_pair, ones, seg = _mlstm_constants()
    return pl.pallas_call(
        kern,
        grid=(nseq, nchunks),
        in_specs=[pl.BlockSpec((CHUNK, 2 * W_M), lambda b, c: (tok(b, c), Z_QK // (2 * W_M))),
                  pl.BlockSpec((CHUNK, W_M), lambda b, c: (tok(b, c), Z_V // W_M)),
                  pl.BlockSpec((CHUNK, W_M), lambda b, c: (tok(b, c), Z_O // W_M)),
                  pl.BlockSpec((CHUNK, LANE), lambda b, c: (tok(b, c), Z_GI // LANE)),
                  pl.BlockSpec((CHUNK, LANE), lambda b, c: (tok(b, c), Z_GF // LANE)),
                  pl.BlockSpec((CONV_W, 2 * W_M), const2),
                  pl.BlockSpec((1, 2 * W_M), const2),
                  pl.BlockSpec((1, LANE), const2),
                  pl.BlockSpec((1, LANE), const2),
                  pl.BlockSpec((1, W_M), const2),
                  pl.BlockSpec(per_head.shape, const3),
                  pl.BlockSpec(per_pair.shape, const3),
                  pl.BlockSpec(ones.shape, const2),
                  pl.BlockSpec(seg.shape, const2)],
        out_specs=[pl.BlockSpec((CHUNK, W_M), lambda b, c: (tok(b, c), 0)),
                   pl.BlockSpec((1, npair, LANE, LANE), lambda b, c: (b, 0, 0, 0)),
                   pl.BlockSpec((1, 1, W_M), lambda b, c: (b, 0, 0)),
                   pl.BlockSpec((1, 1, LANE), lambda b, c: (b, 0, 0))],
        out_shape=[jax.ShapeDtypeStruct((n, W_M), F32),
                   jax.ShapeDtypeStruct((nseq, npair, LANE, LANE), F32),
                   jax.ShapeDtypeStruct((nseq, 1, W_M), F32),
                   jax.ShapeDtypeStruct((nseq, 1, LANE), F32)],
        scratch_shapes=[pltpu.VMEM((CHUNK + SUBLANE, 2 * W_M), F32),
                        pltpu.VMEM((npair, LANE, LANE), F32),
                        pltpu.VMEM((1, W_M), F32),
                        pltpu.VMEM((1, LANE), F32)],
        compiler_params=_params("parallel", "arbitrary"),
        name="mlstm_prompt",
    )(z, z, z, z, z, cw, cb, b_i, b_f, mhg, per_head, per_pair, ones, seg)


def _mlstm_sample_kernel(uq_ref, uk_ref, v_ref, o_ref, g_ref, pq_ref, pk_ref,
                         cwq_ref, cwk_ref, cbq_ref, cbk_ref, bif_ref, mhg_ref,
                         c_ref, n_ref, m_ref,
                         hm_ref, cout_ref, nout_ref, mout_ref,
                         q_s, k_s, vw_s, qc_s, *, tokens):
    def conv(u_ref, p_ref, cw_ref, cb_ref):
        rows = [p_ref[j] for j in range(CONV_W - 1)] + [u_ref[t] for t in range(tokens)]
        out = []
        for t in range(tokens):
            y = cb_ref[...]
            for j in range(CONV_W):
                y = y + rows[t + j] * cw_ref[j]
            out.append(y * jax.nn.sigmoid(y))
        return out

    for t, (qt, kt) in enumerate(zip(conv(uq_ref, pq_ref, cwq_ref, cbq_ref),
                                     conv(uk_ref, pk_ref, cwk_ref, cbk_ref))):
        q_s[t] = qt
        k_s[t] = kt * HEAD_DIM ** -0.5

    ig = [g_ref[0, t, 0:1, :] + bif_ref[0, 0:1, :] for t in range(tokens)]
    lf = [_log_sigmoid(g_ref[0, t, 1:2, :] + bif_ref[0, 1:2, :]) for t in range(tokens)]
    b = [lf[0]]
    for t in range(1, tokens):
        b.append(b[-1] + lf[t])
    m0 = m_ref[0]
    nvec = n_ref[0]
    a = [bt + m0 for bt in b]
    dmat = [[b[t] - b[s] + ig[s] for s in range(t + 1)] for t in range(tokens)]
    m_t = [functools.reduce(jnp.maximum, dmat[t], a[t]) for t in range(tokens)]
    inter = [jnp.exp(a[t] - m_t[t]) for t in range(tokens)]
    smat = [[jnp.sum(q_s[t] * k_s[s], axis=0, keepdims=True) * jnp.exp(dmat[t][s] - m_t[t])
             for s in range(t + 1)] for t in range(tokens)]
    den = [inter[t] * jnp.sum(q_s[t] * nvec, axis=0, keepdims=True)
           + functools.reduce(lambda x, y: x + y, smat[t]) for t in range(tokens)]
    dnm = [jnp.maximum(jnp.abs(den[t]), jnp.exp(-m_t[t])) for t in range(tokens)]
    m_last = m_t[-1]
    w = [jnp.exp(b[-1] - b[s] + ig[s] - m_last) for s in range(tokens)]
    decay = jnp.exp(a[-1] - m_last)
    for s in range(tokens):
        vw_s[s] = v_ref[s] * w[s]

    def per_vdim(f, carry):
        row = pl.ds(f, 1)
        cf = c_ref[0, f]
        for t in range(tokens):
            qc_s[t, row, :] = jnp.sum(q_s[t] * cf, axis=0, keepdims=True)
        upd = decay * cf
        for s in range(tokens):
            upd = upd + vw_s[s, row, :] * k_s[s]
        cout_ref[0, f] = upd
        return carry
    lax.fori_loop(0, HEAD_DIM, per_vdim, 0)

    n_new = decay * nvec
    for s in range(tokens):
        n_new = n_new + w[s] * k_s[s]
    nout_ref[0] = n_new
    mout_ref[0] = m_last
    for t in range(tokens):
        num = inter[t] * qc_s[t]
        for s in range(t + 1):
            num = num + smat[t][s] * v_ref[s]
        hv = num / dnm[t] * jax.nn.sigmoid(o_ref[t])
        ms = jnp.mean(hv * hv, axis=0, keepdims=True)
        hm_ref[t] = hv * lax.rsqrt(ms + EPS) * mhg_ref[...]


def _mlstm_sample(u_t, g_t, cprev_t, cw_b, cb_b, bif_b, mhg_b, c_t, n_t, m_t):
    tokens, _, nseq = u_t.shape
    e = HEAD_DIM
    tok_blk = lambda sec: pl.BlockSpec((tokens, e, nseq), lambda h, sec=sec: (0, sec * H_M + h, 0))
    prev_blk = lambda sec: pl.BlockSpec((CONV_W - 1, e, nseq), lambda h, sec=sec: (0, sec * H_M + h, 0))
    cw_blk = lambda sec: pl.BlockSpec((CONV_W, e, nseq), lambda h, sec=sec: (0, sec * H_M + h, 0))
    cb_blk = lambda sec: pl.BlockSpec((e, nseq), lambda h, sec=sec: (sec * H_M + h, 0))
    c_blk = pl.BlockSpec((1, e, e, nseq), lambda h: (h, 0, 0, 0))
    n_blk = pl.BlockSpec((1, e, nseq), lambda h: (h, 0, 0))
    m_blk = pl.BlockSpec((1, 1, nseq), lambda h: (h, 0, 0))
    kern = functools.partial(_mlstm_sample_kernel, tokens=tokens)
    return pl.pallas_call(
        kern,
        grid=(H_M,),
        in_specs=[tok_blk(0), tok_blk(1), tok_blk(2), tok_blk(3),
                  pl.BlockSpec((1, tokens, 2, nseq), lambda h: (h, 0, 0, 0)),
                  prev_blk(0), prev_blk(1), cw_blk(0), cw_blk(1), cb_blk(0), cb_blk(1),
                  pl.BlockSpec((1, 2, nseq), lambda h: (h, 0, 0)),
                  pl.BlockSpec((e, nseq), lambda h: (h, 0)),
                  c_blk, n_blk, m_blk],
        out_specs=[pl.BlockSpec((tokens, e, nseq), lambda h: (0, h, 0)), c_blk, n_blk, m_blk],
        out_shape=[jax.ShapeDtypeStruct((tokens, W_M, nseq), F32),
                   jax.ShapeDtypeStruct(c_t.shape, F32),
                   jax.ShapeDtypeStruct(n_t.shape, F32),
                   jax.ShapeDtypeStruct(m_t.shape, F32)],
        scratch_shapes=[pltpu.VMEM((tokens, e, nseq), F32) for _ in range(4)],
        compiler_params=_params("parallel"),
        name="mlstm_sample",
    )(u_t, u_t, u_t, u_t, g_t, cprev_t, cprev_t, cw_b, cw_b, cb_b, cb_b, bif_b, mhg_b, c_t, n_t, m_t)


def _from_blockdiag(cbd):
    lead = cbd.shape[:-3]
    c0 = cbd[..., :HEAD_DIM, :HEAD_DIM]
    c1 = cbd[..., HEAD_DIM:, HEAD_DIM:]
    return jnp.stack([c0, c1], axis=-3).reshape(lead + (H_M, HEAD_DIM, HEAD_DIM))


ITEMS_PER_STAGE = (3, 2, 1)


def _pipelined_loop(items, stages):
    depth = len(stages)
    group = next(g for g in ITEMS_PER_STAGE if items % g == 0)
    count = items // group

    def tick(t, static):
        for s in reversed(range(depth)):
            if not static or 0 <= t - s < count:
                for u in range(group):
                    stages[s]((t - s) * group + u)

    if count < depth:
        for t in range(count + depth - 1):
            tick(t, True)
        return
    for t in range(depth - 1):
        tick(t, True)

    def body(t, carry):
        tick(t, False)
        return carry
    lax.fori_loop(depth - 1, count, body, 0)
    for t in range(count, count + depth - 1):
        tick(t, True)

def _attn_prompt_kernel(q_ref, k_ref, v_ref, tab_ref, out_ref, obuf, lbuf, lg_s, p_s, s_s, l_s,
                        *, seq):
    col = lax.broadcasted_iota(jnp.int32, (CHUNK, LANE), 1)
    lane_lo = col < HEAD_DIM

    for bi, (win, dil) in enumerate(DIL_BRANCHES):
        assert win // dil == CHUNK
        nfirst = dil
        nrest = seq // CHUNK - dil
        assert 2 * (nfirst + nrest) <= lg_s.shape[0]

        def rows(st, dil=dil):
            return pl.ds(st, CHUNK) if dil == 1 else pl.ds(st, CHUNK, stride=dil)

        def place(i, with_prev, dil=dil, nfirst=nfirst):
            if not with_prev:
                return i, 2 * i, CHUNK
            start = i % dil + dil * CHUNK * (1 + i // dil)
            if dil == 1 and not isinstance(start, int):
                start = pl.multiple_of(start, CHUNK)
            return start, 2 * (nfirst + i), 2 * CHUNK

        def window(ref, start, with_prev, dil=dil, rows=rows):
            w = ref[rows(start), :]
            if with_prev:
                w = jnp.concatenate([ref[rows(start - dil * CHUNK), :], w], axis=0)
            return w.astype(BF16)

        def logits(i, with_prev, bi=bi, rows=rows, place=place, window=window):
            start, slot, width = place(i, with_prev)
            qb = q_ref[rows(start), :]
            kw = window(k_ref, start, with_prev)
            for hh in range(2):
                sel = lane_lo if hh == 0 else jnp.logical_not(lane_lo)
                qm = jnp.where(sel, qb, 0.0).astype(BF16)
                lg_s[slot + hh, :, 0:width] = (_dot_nt(qm, kw)
                                               + tab_ref[bi, hh, :, 2 * CHUNK - width:])

        def rowmax(i, with_prev, place=place):
            _, slot, width = place(i, with_prev)
            for j in (slot, slot + 1):
                mx = jnp.max(lg_s[j, :, 0:width], axis=1, keepdims=True)
                l_s[j] = jnp.broadcast_to(mx, (CHUNK, LANE))

        def softmax(i, with_prev, place=place):
            _, slot, width = place(i, with_prev)
            for j in (slot, slot + 1):
                mx = l_s[j]
                pr = jnp.concatenate([jnp.exp(lg_s[j, :, k:k + LANE] - mx)
                                      for k in range(0, width, LANE)], axis=1)
                sm = jnp.broadcast_to(jnp.sum(pr, axis=1, keepdims=True), (CHUNK, LANE))
                p_s[j, :, 0:width] = pr.astype(BF16)
                s_s[j] = sm
                l_s[j] = mx + jnp.log(sm)

        def weighted(i, with_prev, bi=bi, rows=rows, place=place, window=window):
            start, slot, width = place(i, with_prev)
            vw = window(v_ref, start, with_prev)
            pv0 = jnp.dot(p_s[slot, :, 0:width], vw, preferred_element_type=F32)
            pv1 = jnp.dot(p_s[slot + 1, :, 0:width], vw, preferred_element_type=F32)
            obuf[bi, rows(start), :] = jnp.where(lane_lo, pv0 / s_s[slot], pv1 / s_s[slot + 1])
            lbuf[bi, rows(start), :] = jnp.where(lane_lo, l_s[slot], l_s[slot + 1])

        for with_prev, count in ((False, nfirst), (True, nrest)):
            _pipelined_loop(count, [functools.partial(stage, with_prev=with_prev)
                                    for stage in (logits, rowmax, softmax, weighted)])

    step = 2 * CHUNK

    def combine(i, carry):
        r = pl.ds(pl.multiple_of(i * step, step), step)
        l0, l1, l2 = lbuf[0, r, :], lbuf[1, r, :], lbuf[2, r, :]
        mx = jnp.maximum(jnp.maximum(l0, l1), l2)
        e0, e1, e2 = jnp.exp(l0 - mx), jnp.exp(l1 - mx), jnp.exp(l2 - mx)
        den = e0 + e1 + e2
        out_ref[r, :] = ((e0 / den) * obuf[0, r, :] + (e1 / den) * obuf[1, r, :]
                         + (e2 / den) * obuf[2, r, :])
        return carry
    lax.fori_loop(0, seq // step, combine, 0)


def _attn_prompt(z, tab, *, nseq, seq):
    n = z.shape[0]
    npair = H_A // 2
    nb = len(DIL_BRANCHES)
    nslot = 2 * (seq // CHUNK)
    kern = functools.partial(_attn_prompt_kernel, seq=seq)
    return pl.pallas_call(
        kern,
        grid=(nseq, npair),
        in_specs=[pl.BlockSpec((seq, LANE), lambda b, p: (b, Z_AQ // LANE + p)),
                  pl.BlockSpec((seq, LANE), lambda b, p: (b, Z_AK // LANE + p)),
                  pl.BlockSpec((seq, LANE), lambda b, p: (b, Z_AV // LANE + p)),
                  pl.BlockSpec((nb, 2, CHUNK, 2 * CHUNK), lambda b, p: (0, p, 0, 0))],
        out_specs=pl.BlockSpec((seq, LANE), lambda b, p: (b, p)),
        out_shape=jax.ShapeDtypeStruct((n, W_A), F32),
        scratch_shapes=[pltpu.VMEM((nb, seq, LANE), F32),
                        pltpu.VMEM((nb, seq, LANE), F32),
                        pltpu.VMEM((nslot, CHUNK, 2 * CHUNK), F32),
                        pltpu.VMEM((nslot, CHUNK, 2 * CHUNK), BF16),
                        pltpu.VMEM((nslot, CHUNK, LANE), F32),
                        pltpu.VMEM((nslot, CHUNK, LANE), F32)],
        compiler_params=_params("parallel", "parallel"),
        name="attn_prompt",
    )(z, z, z, tab)


def _prompt_bias_table(rel_bias):
    qi = jnp.arange(CHUNK)[:, None]
    kj = jnp.arange(2 * CHUNK)[None, :]
    delta = qi + CHUNK - kj
    tabs = []
    for win, dil in DIL_BRANCHES:
        wc = win // dil
        valid = (delta >= 0) & (delta <= wc)
        bias = _bias_lookup(rel_bias, _rel_bucket(jnp.clip(delta, 0) * dil))
        tabs.append(jnp.where(valid[None], bias, NEG))
    return jnp.stack(tabs)


def _bias_lookup(rel_bias, bucket):
    onehot = (bucket[..., None] == jnp.arange(N_BUCKETS)).astype(F32)
    return jnp.einsum("...n,nh->h...", onehot, rel_bias.astype(F32),
                      precision=lax.Precision.HIGHEST)


def _bdot(a, b, contract):
    return lax.dot_general(a, b, ((contract[0], contract[1]), ((0,), (0,))),
                           preferred_element_type=F32)


def _attn_sample_kernel(li_ref, q_ref, kn_ref, vn_ref, kt_ref, vt_ref, *rest, tokens, wbuf):
    del li_ref
    nb = len(DIL_BRANCHES)
    tab_refs, tabn_ref, out_ref = rest[:nb], rest[nb], rest[nb + 1]
    q = q_ref[0].astype(BF16)
    kt = kt_ref[0, 0].astype(BF16)
    vt = vt_ref[0, 0].astype(BF16)
    pad = jnp.zeros((H_A, LANE - tokens, HEAD_DIM), F32)
    kn = jnp.concatenate([kn_ref[0], pad], axis=1).astype(BF16)
    vn = jnp.concatenate([vn_ref[0], pad], axis=1).astype(BF16)
    lg_cache = _bdot(q, kt, ((2,), (1,)))
    lg_new = _bdot(q, kn, ((2,), (2,)))

    probs, probs_new, stats = [], [], []
    for bi, (win, dil) in enumerate(DIL_BRANCHES):
        lo = wbuf - win
        lg = lg_cache[:, :, lo:] + tab_refs[bi][...]
        ln = lg_new + tabn_ref[bi]
        mx = jnp.maximum(jnp.max(lg, axis=-1, keepdims=True), jnp.max(ln, axis=-1, keepdims=True))
        pr = jnp.exp(lg - mx)
        pn = jnp.exp(ln - mx)
        sm = jnp.sum(pr, axis=-1, keepdims=True) + jnp.sum(pn, axis=-1, keepdims=True)
        if lo > 0:
            pr = jnp.concatenate([jnp.zeros((H_A, tokens, lo), F32), pr], axis=-1)
        probs.append(pr)
        probs_new.append(pn)
        stats.append((mx, sm))
    p_all = jnp.concatenate(probs, axis=1).astype(BF16)
    pn_all = jnp.concatenate(probs_new, axis=1).astype(BF16)
    o_all = _bdot(p_all, vt, ((2,), (2,))) + _bdot(pn_all, vn, ((2,), (1,)))

    lses = [mx + jnp.log(sm) for mx, sm in stats]
    top = functools.reduce(jnp.maximum, lses)
    es = [jnp.exp(l - top) for l in lses]
    den = functools.reduce(lambda a, b: a + b, es)
    out = None
    for bi in range(nb):
        term = (es[bi] / den) * (o_all[:, bi * tokens:(bi + 1) * tokens] / stats[bi][1])
        out = term if out is None else out + term
    out_ref[0] = out


def _attn_sample(layer, q, kn, vn, cache_kt, cache_vt, tabs, tabn):
    nseq, _, tokens, _ = q.shape
    wbuf = cache_kt.shape[-1]
    new_spec = pl.BlockSpec((1, H_A, tokens, HEAD_DIM), lambda b, li: (b, 0, 0, 0))
    cache_spec = pl.BlockSpec((1, 1, H_A, HEAD_DIM, wbuf), lambda b, li: (li[0], b, 0, 0, 0))
    tab_specs = [pl.BlockSpec(t.shape, lambda b, li: (0, 0, 0)) for t in tabs]
    kern = functools.partial(_attn_sample_kernel, tokens=tokens, wbuf=wbuf)
    return pl.pallas_call(
        kern,
        grid_spec=pltpu.PrefetchScalarGridSpec(
            num_scalar_prefetch=1,
            grid=(nseq,),
            in_specs=[new_spec, new_spec, new_spec, cache_spec, cache_spec, *tab_specs,
                      pl.BlockSpec(tabn.shape, lambda b, li: (0, 0, 0, 0))],
            out_specs=new_spec),
        out_shape=jax.ShapeDtypeStruct((nseq, H_A, tokens, HEAD_DIM), F32),
        compiler_params=_params("parallel"),
        name="attn_sample",
    )(layer, q, kn, vn, cache_kt, cache_vt, *tabs, tabn)


def _sample_bias_tables(rel_bias, tokens, wbuf):
    assert wbuf >= max(w for w, _ in DIL_BRANCHES)
    tok = jnp.arange(tokens)[:, None]
    tabs = []
    for win, dil in DIL_BRANCHES:
        pos = jnp.arange(wbuf - win, wbuf)[None, :]
        dist = wbuf + tok - pos
        valid = (dist % dil == 0) & (dist <= win)
        tabs.append(jnp.where(valid[None], _bias_lookup(rel_bias, _rel_bucket(dist)), NEG))
    other = jnp.arange(LANE)[None, :]
    dist = tok - other
    news = []
    for win, dil in DIL_BRANCHES:
        valid = (dist >= 0) & (dist % dil == 0) & (dist <= win)
        news.append(jnp.where(valid[None], _bias_lookup(rel_bias, _rel_bucket(jnp.clip(dist, 0))), NEG))
    return tabs, jnp.stack(news)


def _post_kernel(x_ref, hm_ref, ha_ref, pe_ref, wo_ref, g2_ref, wg_ref, wu_ref, wd_ref, g3_ref,
                 wpg_ref, wpe_ref, out_ref, x1_s, h2_s, acc, *, nff):
    j = pl.program_id(1)

    @pl.when(j == 0)
    def _mix():
        x1 = (x_ref[...]
              + jnp.dot(hm_ref[...].astype(BF16), wo_ref[0:W_M, :], preferred_element_type=F32)
              + jnp.dot(ha_ref[...].astype(BF16), wo_ref[W_M:, :], preferred_element_type=F32))
        x1_s[...] = x1
        ms = jnp.mean(x1 * x1, axis=-1, keepdims=True)
        h2_s[...] = (x1 * lax.rsqrt(ms + EPS) * g2_ref[...]).astype(BF16)
        acc[...] = jnp.zeros(acc.shape, F32)

    h2 = h2_s[...]
    gate = jnp.dot(h2, wg_ref[...], preferred_element_type=F32)
    up = jnp.dot(h2, wu_ref[...], preferred_element_type=F32)
    act = (gate * jax.nn.sigmoid(gate) * up).astype(BF16)
    acc[...] += jnp.dot(act, wd_ref[...], preferred_element_type=F32)

    @pl.when(j == nff - 1)
    def _fin():
        x2 = x1_s[...] + acc[...]
        ms = jnp.mean(x2 * x2, axis=-1, keepdims=True)
        h3 = (x2 * lax.rsqrt(ms + EPS) * g3_ref[...]).astype(BF16)
        pg = jax.nn.sigmoid(jnp.dot(h3, wpg_ref[...], preferred_element_type=F32))
        pp = jnp.dot(pe_ref[...].astype(BF16), wpe_ref[...], preferred_element_type=F32)
        out_ref[...] = x2 + pg * pp


def _post(x, hm, ha, pe, wo, g2, wgu, wd, g3, wpg, wpe):
    n = x.shape[0]
    tm = min(ROWS_POST, n)
    nff = D_FF // FF_BLOCK
    row = lambda i, j: (i, 0)
    fixed = lambda shape: pl.BlockSpec(shape, lambda i, j: (0, 0), pipeline_mode=pl.Buffered(1))
    kern = functools.partial(_post_kernel, nff=nff)
    return pl.pallas_call(
        kern,
        grid=(n // tm, nff),
        in_specs=[pl.BlockSpec((tm, D_MODEL), row),
                  pl.BlockSpec((tm, W_M), row),
                  pl.BlockSpec((tm, W_A), row),
                  pl.BlockSpec((tm, D_PLE), row),
                  fixed((D_MODEL, D_MODEL)),
                  fixed((1, D_MODEL)),
                  pl.BlockSpec((D_MODEL, FF_BLOCK), lambda i, j: (0, j)),
                  pl.BlockSpec((D_MODEL, FF_BLOCK), lambda i, j: (0, nff + j)),
                  pl.BlockSpec((FF_BLOCK, D_MODEL), lambda i, j: (j, 0)),
                  fixed((1, D_MODEL)),
                  fixed((D_MODEL, D_MODEL)),
                  fixed((D_PLE, D_MODEL))],
        out_specs=pl.BlockSpec((tm, D_MODEL), row),
        out_shape=jax.ShapeDtypeStruct((n, D_MODEL), F32),
        scratch_shapes=[pltpu.VMEM((tm, D_MODEL), F32),
                        pltpu.VMEM((tm, D_MODEL), BF16),
                        pltpu.VMEM((tm, D_MODEL), F32)],
        compiler_params=_params("parallel", "arbitrary"),
        name="post",
    )(x, hm, ha, pe, wo, g2, wgu, wgu, wd, g3, wpg, wpe)


def _mixer_tail(x, hm, ha, pe, lw):
    return _post(x, hm, ha, pe, lw["w_out"], lw["norm2_g"], lw["w_gu"], lw["w_down"],
                 lw["ple_norm_g"], lw["w_pg"], lw["w_pe"])


def _layer(xp, xs, lw, *, batch, seq, dec_batch, dec_seq, cache_k, cache_v, tab_p, tab_s, tab_n, seg):
    def proj(x):
        return _in_proj(x, lw["norm1_g"], lw["w_in"], lw["q_norm_g"], lw["k_norm_g"], seg)

    zp = proj(xp)
    hm, pc, pn, pm = _mlstm_prompt(zp, lw["conv_w"], lw["conv_b"], lw["b_i"], lw["b_f"],
                                   lw["mh_norm_g"], nseq=batch, nchunks=seq // CHUNK)
    ha = _attn_prompt(zp, tab_p, nseq=batch, seq=seq)
    xp_new = _mixer_tail(xp, hm, ha, lw["p_prompt"], lw)
    zp3 = zp.reshape(batch, seq, Z_W)
    keep = min(max(w for w, _ in DIL_BRANCHES), seq)
    p_out = (zp3[:, seq - keep:, Z_AK:Z_AK + W_A].reshape(batch, keep, H_A, HEAD_DIM),
             zp3[:, seq - keep:, Z_AV:Z_AV + W_A].reshape(batch, keep, H_A, HEAD_DIM),
             _from_blockdiag(pc),
             pn.reshape(batch, H_M, HEAD_DIM),
             pm[:, 0, :H_M],
             zp3[:, seq - (CONV_W - 1):, Z_QK:Z_QK + 2 * W_M])

    zs = proj(xs)
    zs3 = zs.reshape(dec_batch, dec_seq, Z_W)
    lanes = lambda a: jnp.broadcast_to(a[..., None], a.shape + (dec_batch,))
    gates = jnp.stack([zs3[:, :, Z_GI:Z_GI + H_M], zs3[:, :, Z_GF:Z_GF + H_M]], axis=2)
    hm_t, sc, sn, sm = _mlstm_sample(
        jnp.transpose(zs3[:, :, :Z_AQ], (1, 2, 0)),
        jnp.transpose(gates, (3, 1, 2, 0)),
        jnp.transpose(lw["state_conv"], (1, 2, 0)),
        lanes(lw["conv_w"]), lanes(lw["conv_b"][0]),
        lanes(jnp.stack([lw["b_i"][0, :H_M], lw["b_f"][0, :H_M]], axis=1)), lanes(lw["mh_norm_g"][0]),
        jnp.transpose(lw["state_C"], (1, 2, 3, 0)),
        jnp.transpose(lw["state_n"], (1, 2, 0)),
        jnp.transpose(lw["state_m"], (1, 0))[:, None, :])
    hm = jnp.transpose(hm_t, (2, 0, 1)).reshape(dec_batch * dec_seq, W_M)
    heads = (dec_batch, dec_seq, H_A, HEAD_DIM)
    q_s = zs3[:, :, Z_AQ:Z_AQ + W_A].reshape(heads)
    k_s = zs3[:, :, Z_AK:Z_AK + W_A].reshape(heads)
    v_s = zs3[:, :, Z_AV:Z_AV + W_A].reshape(heads)
    by_head = lambda t: jnp.transpose(t, (0, 2, 1, 3))
    ha = _attn_sample(lw["layer"], by_head(q_s), by_head(k_s), by_head(v_s), cache_k, cache_v,
                      tab_s, tab_n)
    ha = by_head(ha).reshape(dec_batch * dec_seq, W_A)
    xs_new = _mixer_tail(xs, hm, ha, lw["p_sample"], lw)
    conv_rows = jnp.concatenate([lw["state_conv"], zs3[:, :, Z_QK:Z_QK + 2 * W_M]], axis=1)
    s_out = (k_s, v_s,
             jnp.transpose(sc, (3, 0, 1, 2)),
             jnp.transpose(sn, (2, 0, 1)),
             jnp.transpose(sm[:, 0, :], (1, 0)),
             conv_rows[:, -(CONV_W - 1):])
    return xp_new, xs_new, p_out, s_out


def kernel(x_prompt, x_sample, p_prompt, p_sample, cache_attn_k, cache_attn_v, state_mlstm_C, state_mlstm_n, state_mlstm_m, state_conv, rel_bias, norm1_g, w_in, b_if, conv_w, conv_b, mh_norm_g, q_norm_g, k_norm_g, w_out, norm2_g, w_gu, w_down, ple_norm_g, w_pe, w_pg):
    batch, seq, _ = x_prompt.shape
    dec_batch, dec_seq, _ = x_sample.shape
    depth = w_in.shape[0]
    assert seq % (CHUNK * max(d for _, d in DIL_BRANCHES)) == 0

    n_m = 4 * W_M
    gate_pad = jnp.zeros((depth, D_MODEL, LANE - H_M), w_in.dtype)
    w_in_z = jnp.concatenate(
        [w_in[:, :, :n_m], w_in[:, :, n_m + 2 * H_M:],
         w_in[:, :, n_m:n_m + H_M], gate_pad, w_in[:, :, n_m + H_M:n_m + 2 * H_M], gate_pad],
        axis=-1).astype(BF16)
    b_gate = lambda lo: jnp.pad(b_if[:, lo:lo + H_M], ((0, 0), (0, LANE - H_M)))[:, None, :]

    head_of_lane = jnp.arange(W_A) // HEAD_DIM
    seg = (head_of_lane[:, None] == head_of_lane[None, :]).astype(BF16)
    tab_p = _prompt_bias_table(rel_bias)
    tab_s, tab_n = _sample_bias_tables(rel_bias, dec_seq, cache_attn_k.shape[2])
    cache_kt = jnp.transpose(cache_attn_k, (0, 1, 3, 4, 2))
    cache_vt = jnp.transpose(cache_attn_v, (0, 1, 3, 4, 2))

    row = lambda a: a[:, None, :]
    per_layer = dict(
        layer=jnp.arange(depth, dtype=jnp.int32)[:, None],
        norm1_g=row(norm1_g), w_in=w_in_z,
        q_norm_g=row(jnp.tile(q_norm_g, (1, H_A))), k_norm_g=row(jnp.tile(k_norm_g, (1, H_A))),
        conv_w=conv_w, conv_b=row(conv_b),
        b_i=b_gate(0), b_f=b_gate(H_M),
        mh_norm_g=row(mh_norm_g),
        w_out=w_out.astype(BF16), norm2_g=row(norm2_g),
        w_gu=w_gu.astype(BF16), w_down=w_down.astype(BF16),
        ple_norm_g=row(ple_norm_g), w_pg=w_pg.astype(BF16), w_pe=w_pe.astype(BF16),
        p_prompt=p_prompt.reshape(depth, batch * seq, D_PLE),
        p_sample=p_sample.reshape(depth, dec_batch * dec_seq, D_PLE),
        state_conv=state_conv, state_C=state_mlstm_C, state_n=state_mlstm_n, state_m=state_mlstm_m,
    )
    xp = x_prompt.reshape(batch * seq, D_MODEL)
    xs = x_sample.reshape(dec_batch * dec_seq, D_MODEL)
    p_outs, s_outs = [], []
    for i in range(depth):
        lw = {name: val[i] for name, val in per_layer.items()}
        xp, xs, p_out, s_out = _layer(xp, xs, lw, batch=batch, seq=seq, dec_batch=dec_batch,
                                      dec_seq=dec_seq, cache_k=cache_kt, cache_v=cache_vt,
                                      tab_p=tab_p, tab_s=tab_s, tab_n=tab_n, seg=seg)
        p_outs.append(p_out)
        s_outs.append(s_out)
    stack = lambda outs: tuple(jnp.stack(leaf) for leaf in zip(*outs))
    return (xp.reshape(batch, seq, D_MODEL), xs.reshape(dec_batch, dec_seq, D_MODEL),
            *stack(p_outs), *stack(s_outs))
```

```python
import functools
import math

import jax
import jax.numpy as jnp
import numpy as np
from jax import lax
from jax.experimental import pallas as pl
from jax.experimental.pallas import tpu as pltpu

F32 = jnp.float32
BF16 = jnp.bfloat16

D_MODEL = 1024
HEAD_DIM = 64
H_M = 8
H_A = 8
W_M = H_M * HEAD_DIM
W_A = H_A * HEAD_DIM
CONV_W = 4
CHUNK = 128
DIL_BRANCHES = ((128, 1), (512, 4), (2048, 16))
N_BUCKETS = 32
MAX_DIST = 2048
D_FF = 2816
D_PLE = 256
EPS = 1e-6
NEG = -1e30

LANE = 128
SUBLANE = 8
VMEM_LIMIT_BYTES = 56 * 1024 * 1024

Z_QK = 0
Z_V = 2 * W_M
Z_O = 3 * W_M
Z_AQ = 4 * W_M
Z_AK = Z_AQ + W_A
Z_AV = Z_AK + W_A
Z_GI = Z_AV + W_A
Z_GF = Z_GI + LANE
Z_W = Z_GF + LANE

ROWS_IN = 256
ROWS_POST = 512
FF_BLOCK = D_FF // 2


def _params(*sem):
    return pltpu.CompilerParams(dimension_semantics=sem, vmem_limit_bytes=VMEM_LIMIT_BYTES)


def _rel_bucket(dist):
    max_exact = N_BUCKETS // 2
    df = np.maximum(dist, 1).astype(np.float32)
    scaled = (np.log(df / np.float32(max_exact)) / np.float32(math.log(MAX_DIST / max_exact))
              * np.float32(N_BUCKETS - max_exact))
    large = np.minimum(max_exact + scaled.astype(np.int32), N_BUCKETS - 1)
    return np.where(dist < max_exact, dist, large).astype(np.int32)


def _split_dot(p, ones_bf16):
    hi = p.astype(BF16)
    lo = (p - hi.astype(F32)).astype(BF16)
    return (jnp.dot(hi, ones_bf16, preferred_element_type=F32)
            + jnp.dot(lo, ones_bf16, preferred_element_type=F32))


def _dot_nt(a, b):
    return lax.dot_general(a, b, (((1,), (1,)), ((), ())), preferred_element_type=F32)


def _in_proj_kernel(x_ref, g_ref, w_ref, qg_ref, kg_ref, seg_ref, *rest, windows):
    outs = rest[len(rest) - (3 if windows else 1):]
    z_ref = outs[0]
    x = x_ref[...]
    ms = jnp.mean(x * x, axis=-1, keepdims=True)
    h = (x * lax.rsqrt(ms + EPS) * g_ref[...]).astype(BF16)
    z_ref[:, 0:Z_AQ] = jnp.dot(h, w_ref[:, 0:Z_AQ], preferred_element_type=F32)
    for off, gr, mult in ((Z_AQ, qg_ref, HEAD_DIM ** -0.5), (Z_AK, kg_ref, None)):
        a = jnp.dot(h, w_ref[:, off:off + W_A], preferred_element_type=F32)
        ss = _split_dot(a * a, seg_ref[...])
        a = a * lax.rsqrt(ss * (1.0 / HEAD_DIM) + EPS) * gr[...]
        z_ref[:, off:off + W_A] = a if mult is None else a * mult
        if windows and off == Z_AK:
            outs[1][...] = a.T
    tail = jnp.dot(h, w_ref[:, Z_AV:Z_W], preferred_element_type=F32)
    z_ref[:, Z_AV:Z_W] = tail
    if windows:
        outs[2][...] = tail[:, 0:W_A].T


def _in_proj(x, g, w, qg, kg, seg, windows=None):
    n = x.shape[0]
    tm = min(ROWS_IN, n)
    const = lambda i: (0, 0)
    in_specs = [pl.BlockSpec((tm, D_MODEL), lambda i: (i, 0)),
                pl.BlockSpec((1, D_MODEL), const),
                pl.BlockSpec((D_MODEL, Z_W), const),
                pl.BlockSpec((1, W_A), const),
                pl.BlockSpec((1, W_A), const),
                pl.BlockSpec((W_A, W_A), const)]
    out_specs = [pl.BlockSpec((tm, Z_W), lambda i: (i, 0))]
    out_shape = [jax.ShapeDtypeStruct((n, Z_W), F32)]
    args = [x, g, w, qg, kg, seg]
    aliases = {}
    if windows is not None:
        layer, depth, nseq, seq, kt, vt = windows
        per_seq = seq // tm
        win_spec = pl.BlockSpec((None, None, W_A, tm),
                                lambda i: (layer, i // per_seq, 0, i % per_seq))
        out_specs += [win_spec, win_spec]
        out_shape += [jax.ShapeDtypeStruct((depth, nseq, W_A, seq), F32)] * 2
        if kt is not None:
            aliases = {len(args): 1, len(args) + 1: 2}
            in_specs += [pl.BlockSpec(memory_space=pl.ANY)] * 2
            args += [kt, vt]
    return pl.pallas_call(
        functools.partial(_in_proj_kernel, windows=windows is not None),
        grid=(n // tm,),
        in_specs=in_specs,
        out_specs=out_specs,
        out_shape=out_shape,
        input_output_aliases=aliases,
        compiler_params=_params("parallel"),
        name="in_proj",
    )(*args)


def _log_sigmoid(x):
    return jnp.minimum(x, 0.0) - jnp.log1p(jnp.exp(-jnp.abs(x)))


def _scan_rows(x, combine, identity):
    row = lax.broadcasted_iota(jnp.int32, x.shape, 0)
    k = 1
    while k < x.shape[0]:
        x = combine(x, jnp.where(row >= k, pltpu.roll(x, k, axis=0), identity))
        k *= 2
    return x


def _bf16_parts(x, count):
    parts = []
    for _ in range(count):
        part = x.astype(BF16)
        parts.append(part)
        x = x - part.astype(F32)
    return parts


def _expand(parts, ones_bf16):
    out = None
    for part in parts:
        term = jnp.dot(part, ones_bf16, preferred_element_type=F32)
        out = term if out is None else out + term
    return out


def _mlstm_prompt_kernel(u_ref, v_ref, o_ref, gi_ref, gf_ref, cw_ref, cb_ref, bi_ref, bf_ref, mhg_ref,
                         exh_ref, exp_ref, ones_ref, seg_ref,
                         hm_ref, cout_ref, nout_ref, mout_ref,
                         ubuf, cbd, nst, mst, *, nchunks):
    c = pl.program_id(1)

    @pl.when(c == 0)
    def _init():
        ubuf[0:SUBLANE, :] = jnp.zeros((SUBLANE, 2 * W_M), F32)
        cbd[...] = jnp.zeros(cbd.shape, F32)
        nst[...] = jnp.zeros(nst.shape, F32)
        mst[...] = jnp.zeros(mst.shape, F32)

    ubuf[SUBLANE:SUBLANE + CHUNK, :] = u_ref[...]

    def conv_silu(lo):
        y = cb_ref[:, lo:lo + LANE]
        for j in range(CONV_W):
            off = SUBLANE - (CONV_W - 1) + j
            y = y + ubuf[off:off + CHUNK, lo:lo + LANE] * cw_ref[j:j + 1, lo:lo + LANE]
        return y * jax.nn.sigmoid(y)

    row = lax.broadcasted_iota(jnp.int32, (CHUNK, LANE), 0)
    col = lax.broadcasted_iota(jnp.int32, (CHUNK, LANE), 1)
    lane_lo = col < HEAD_DIM
    causal = row >= col
    same_head = (row < HEAD_DIM) == lane_lo
    last = slice(CHUNK - 1, CHUNK)

    b = _scan_rows(_log_sigmoid(gf_ref[...] + bf_ref[...]), jnp.add, 0.0)
    g = gi_ref[...] + bi_ref[...] - b
    m_prev = mst[...]
    big_m = jnp.maximum(m_prev, _scan_rows(g, jnp.maximum, -jnp.inf))
    m_t = b + big_m
    mst[...] = m_t[last, :]
    g_t = g.T
    big_m_parts = _bf16_parts(big_m, 3)
    inter_parts = _bf16_parts(jnp.exp(m_prev - big_m), 2)
    floor_parts = _bf16_parts(jnp.exp(-m_t), 2)
    w_end_parts = _bf16_parts(jnp.exp(g - big_m[last, :]), 2)

    for p in range(H_M // 2):
        sl = slice(p * LANE, (p + 1) * LANE)
        big_m_2 = _expand(big_m_parts, exh_ref[p])
        big_m_h = (big_m_2[:, 0:LANE], big_m_2[:, LANE:])
        inter_p = _expand(inter_parts, exp_ref[p])
        floor_p = _expand(floor_parts, exp_ref[p])
        w_p = _expand(w_end_parts, exp_ref[p])

        q_p = conv_silu(p * LANE)
        k_p = conv_silu(W_M + p * LANE) * HEAD_DIM ** -0.5
        q_bf = q_p.astype(BF16)
        k_bf = k_p.astype(BF16)
        v_p = v_ref[:, sl]
        n_p = nst[:, sl]
        c_old = cbd[p]

        s_heads = []
        for hh in range(2):
            h = 2 * p + hh
            sel = lane_lo if hh == 0 else jnp.logical_not(lane_lo)
            qm = jnp.where(sel, q_p, 0.0).astype(BF16)
            decay_mat = jnp.exp(jnp.where(causal, g_t[h:h + 1, :] - big_m_h[hh], NEG))
            s_heads.append((_dot_nt(qm, k_bf) * decay_mat).astype(BF16))
        s_cat = jnp.concatenate(s_heads, axis=1)
        v_cat = jnp.concatenate([jnp.where(lane_lo, v_p, 0.0), jnp.where(lane_lo, 0.0, v_p)], axis=0)
        intra = jnp.dot(s_cat, jnp.concatenate([v_cat.astype(BF16), ones_ref[...]], axis=1),
                        preferred_element_type=F32)
        n_mat = jnp.where(same_head, jnp.broadcast_to(n_p, (LANE, LANE)), 0.0)
        carried = _dot_nt(q_bf, jnp.concatenate([c_old, n_mat], axis=0).astype(BF16))
        num = inter_p * carried[:, 0:LANE] + intra[:, 0:LANE]
        den = inter_p * carried[:, LANE:] + intra[:, LANE:]
        hv = num / jnp.maximum(jnp.abs(den), floor_p)
        hv = hv * jax.nn.sigmoid(o_ref[:, sl])
        ms = _split_dot(hv * hv, seg_ref[...]) * (1.0 / HEAD_DIM)
        hm_ref[:, sl] = hv * lax.rsqrt(ms + EPS) * mhg_ref[:, sl]

        dec_p = inter_p[last, :]
        c_upd = jnp.dot((v_p * w_p).T.astype(BF16), k_bf, preferred_element_type=F32)
        cbd[p] = dec_p * c_old + jnp.where(same_head, c_upd, 0.0)
        nst[:, sl] = dec_p * n_p + jnp.sum(k_p * w_p, axis=0, keepdims=True)
    ubuf[0:SUBLANE, :] = ubuf[CHUNK:CHUNK + SUBLANE, :]

    @pl.when(c == nchunks - 1)
    def _fin():
        cout_ref[0] = cbd[...]
        nout_ref[0] = nst[...]
        mout_ref[0] = mst[...]


def _mlstm_constants():
    npair = H_M // 2
    half = np.arange(LANE) // HEAD_DIM
    per_head = np.zeros((npair, LANE, 2 * LANE), np.float32)
    per_pair = np.zeros((npair, LANE, LANE), np.float32)
    for p in range(npair):
        per_head[p, 2 * p, 0:LANE] = 1.0
        per_head[p, 2 * p + 1, LANE:] = 1.0
        per_pair[p, 2 * p + half, np.arange(LANE)] = 1.0
    ones = np.zeros((2 * CHUNK, LANE), np.float32)
    ones[:CHUNK, :HEAD_DIM] = 1.0
    ones[CHUNK:, HEAD_DIM:] = 1.0
    seg = (half[:, None] == half[None, :]).astype(np.float32)
    return tuple(jnp.asarray(a, BF16) for a in (per_head, per_pair, ones, seg))


def _mlstm_prompt(z, cw, cb, b_i, b_f, mhg, *, nseq, nchunks):
    n = z.shape[0]
    tok = lambda b, c: b * nchunks + c
    const2 = lambda b, c: (0, 0)
    const3 = lambda b, c: (0, 0, 0)
    kern = functools.partial(_mlstm_prompt_kernel, nchunks=nchunks)
    npair = H_M // 2
    per_head, per_pair, ones, seg = _mlstm_constants()
    return pl.pallas_call(
        kern,
        grid=(nseq, nchunks),
        in_specs=[pl.BlockSpec((CHUNK, 2 * W_M), lambda b, c: (tok(b, c), Z_QK // (2 * W_M))),
                  pl.BlockSpec((CHUNK, W_M), lambda b, c: (tok(b, c), Z_V // W_M)),
                  pl.BlockSpec((CHUNK, W_M), lambda b, c: (tok(b, c), Z_O // W_M)),
                  pl.BlockSpec((CHUNK, LANE), lambda b, c: (tok(b, c), Z_GI // LANE)),
                  pl.BlockSpec((CHUNK, LANE), lambda b, c: (tok(b, c), Z_GF // LANE)),
                  pl.BlockSpec((CONV_W, 2 * W_M), const2),
                  pl.BlockSpec((1, 2 * W_M), const2),
                  pl.BlockSpec((1, LANE), const2),
                  pl.BlockSpec((1, LANE), const2),
                  pl.BlockSpec((1, W_M), const2),
                  pl.BlockSpec(per_head.shape, const3),
                  pl.BlockSpec(per_pair.shape, const3),
                  pl.BlockSpec(ones.shape, const2),
                  pl.BlockSpec(seg.shape, const2)],
        out_specs=[pl.BlockSpec((CHUNK, W_M), lambda b, c: (tok(b, c), 0)),
                   pl.BlockSpec((1, npair, LANE, LANE), lambda b, c: (b, 0, 0, 0)),
                   pl.BlockSpec((1, 1, W_M), lambda b, c: (b, 0, 0)),
                   pl.BlockSpec((1, 1, LANE), lambda b, c: (b, 0, 0))],
        out_shape=[jax.ShapeDtypeStruct((n, W_M), F32),
                   jax.ShapeDtypeStruct((nseq, npair, LANE, LANE), F32),
                   jax.ShapeDtypeStruct((nseq, 1, W_M), F32),
                   jax.ShapeDtypeStruct((nseq, 1, LANE), F32)],
        scratch_shapes=[pltpu.VMEM((CHUNK + SUBLANE, 2 * W_M), F32),
                        pltpu.VMEM((npair, LANE, LANE), F32),
                        pltpu.VMEM((1, W_M), F32),
                        pltpu.VMEM((1, LANE), F32)],
        compiler_params=_params("parallel", "arbitrary"),
        name="mlstm_prompt",
    )(z, z, z, z, z, cw, cb, b_i, b_f, mhg, per_head, per_pair, ones, seg)


def _mlstm_sample_kernel(uq_ref, uk_ref, v_ref, o_ref, g_ref, pq_ref, pk_ref,
                         cwq_ref, cwk_ref, cbq_ref, cbk_ref, bif_ref, mhg_ref,
                         c_ref, n_ref, m_ref, *rest, tokens):
    hm_ref, cout_ref, nout_ref, mout_ref, q_s, k_s, vw_s, qc_s = rest[len(rest) - 8:]
    def conv(u_ref, p_ref, cw_ref, cb_ref):
        rows = [p_ref[j] for j in range(CONV_W - 1)] + [u_ref[t] for t in range(tokens)]
        out = []
        for t in range(tokens):
            y = cb_ref[...]
            for j in range(CONV_W):
                y = y + rows[t + j] * cw_ref[j]
            out.append(y * jax.nn.sigmoid(y))
        return out

    for t, (qt, kt) in enumerate(zip(conv(uq_ref, pq_ref, cwq_ref, cbq_ref),
                                     conv(uk_ref, pk_ref, cwk_ref, cbk_ref))):
        q_s[t] = qt
        k_s[t] = kt * HEAD_DIM ** -0.5

    ig = [g_ref[0, t, 0:1, :] + bif_ref[0, 0:1, :] for t in range(tokens)]
    lf = [_log_sigmoid(g_ref[0, t, 1:2, :] + bif_ref[0, 1:2, :]) for t in range(tokens)]
    b = [lf[0]]
    for t in range(1, tokens):
        b.append(b[-1] + lf[t])
    m0 = m_ref[0]
    nvec = n_ref[0]
    a = [bt + m0 for bt in b]
    dmat = [[b[t] - b[s] + ig[s] for s in range(t + 1)] for t in range(tokens)]
    m_t = [functools.reduce(jnp.maximum, dmat[t], a[t]) for t in range(tokens)]
    inter = [jnp.exp(a[t] - m_t[t]) for t in range(tokens)]
    smat = [[jnp.sum(q_s[t] * k_s[s], axis=0, keepdims=True) * jnp.exp(dmat[t][s] - m_t[t])
             for s in range(t + 1)] for t in range(tokens)]
    den = [inter[t] * jnp.sum(q_s[t] * nvec, axis=0, keepdims=True)
           + functools.reduce(lambda x, y: x + y, smat[t]) for t in range(tokens)]
    dnm = [jnp.maximum(jnp.abs(den[t]), jnp.exp(-m_t[t])) for t in range(tokens)]
    m_last = m_t[-1]
    w = [jnp.exp(b[-1] - b[s] + ig[s] - m_last) for s in range(tokens)]
    decay = jnp.exp(a[-1] - m_last)
    for s in range(tokens):
        vw_s[s] = v_ref[s] * w[s]

    def per_vdim(f, carry):
        row = pl.ds(f, 1)
        cf = c_ref[0, f]
        for t in range(tokens):
            qc_s[t, row, :] = jnp.sum(q_s[t] * cf, axis=0, keepdims=True)
        upd = decay * cf
        for s in range(tokens):
            upd = upd + vw_s[s, row, :] * k_s[s]
        cout_ref[0, f] = upd
        return carry
    lax.fori_loop(0, HEAD_DIM, per_vdim, 0)

    n_new = decay * nvec
    for s in range(tokens):
        n_new = n_new + w[s] * k_s[s]
    nout_ref[0] = n_new
    mout_ref[0] = m_last
    for t in range(tokens):
        num = inter[t] * qc_s[t]
        for s in range(t + 1):
            num = num + smat[t][s] * v_ref[s]
        hv = num / dnm[t] * jax.nn.sigmoid(o_ref[t])
        ms = jnp.mean(hv * hv, axis=0, keepdims=True)
        hm_ref[t] = hv * lax.rsqrt(ms + EPS) * mhg_ref[...]


def _mlstm_sample(u_t, g_t, cprev_t, cw_b, cb_b, bif_b, mhg_b, layer, c_all, c_new, n_t, m_t):
    tokens, _, nseq = u_t.shape
    e = HEAD_DIM
    args = [u_t, u_t, u_t, u_t, g_t, cprev_t, cprev_t, cw_b, cw_b, cb_b, cb_b, bif_b, mhg_b,
            c_all, n_t, m_t]
    extra_specs, aliases = [], {}
    if c_new is not None:
        aliases = {len(args): 1}
        extra_specs = [pl.BlockSpec(memory_space=pl.ANY)]
        args.append(c_new)
    tok_blk = lambda sec: pl.BlockSpec((tokens, e, nseq), lambda h, sec=sec: (0, sec * H_M + h, 0))
    prev_blk = lambda sec: pl.BlockSpec((CONV_W - 1, e, nseq), lambda h, sec=sec: (0, sec * H_M + h, 0))
    cw_blk = lambda sec: pl.BlockSpec((CONV_W, e, nseq), lambda h, sec=sec: (0, sec * H_M + h, 0))
    cb_blk = lambda sec: pl.BlockSpec((e, nseq), lambda h, sec=sec: (sec * H_M + h, 0))
    c_blk = pl.BlockSpec((None, 1, e, e, nseq), lambda h: (layer, h, 0, 0, 0))
    n_blk = pl.BlockSpec((1, e, nseq), lambda h: (h, 0, 0))
    m_blk = pl.BlockSpec((1, 1, nseq), lambda h: (h, 0, 0))
    kern = functools.partial(_mlstm_sample_kernel, tokens=tokens)
    return pl.pallas_call(
        kern,
        grid=(H_M,),
        in_specs=[tok_blk(0), tok_blk(1), tok_blk(2), tok_blk(3),
                  pl.BlockSpec((1, tokens, 2, nseq), lambda h: (h, 0, 0, 0)),
                  prev_blk(0), prev_blk(1), cw_blk(0), cw_blk(1), cb_blk(0), cb_blk(1),
                  pl.BlockSpec((1, 2, nseq), lambda h: (h, 0, 0)),
                  pl.BlockSpec((e, nseq), lambda h: (h, 0)),
                  c_blk, n_blk, m_blk] + extra_specs,
        out_specs=[pl.BlockSpec((tokens, e, nseq), lambda h: (0, h, 0)), c_blk, n_blk, m_blk],
        out_shape=[jax.ShapeDtypeStruct((tokens, W_M, nseq), F32),
                   jax.ShapeDtypeStruct(c_all.shape, F32),
                   jax.ShapeDtypeStruct(n_t.shape, F32),
                   jax.ShapeDtypeStruct(m_t.shape, F32)],
        input_output_aliases=aliases,
        scratch_shapes=[pltpu.VMEM((tokens, e, nseq), F32) for _ in range(4)],
        compiler_params=_params("parallel"),
        name="mlstm_sample",
    )(*args)


def _from_blockdiag(cbd):
    lead = cbd.shape[:-3]
    c0 = cbd[..., :HEAD_DIM, :HEAD_DIM]
    c1 = cbd[..., HEAD_DIM:, HEAD_DIM:]
    return jnp.stack([c0, c1], axis=-3).reshape(lead + (H_M, HEAD_DIM, HEAD_DIM))


ITEMS_PER_STAGE = (3, 2, 1)


def _pipelined_loop(items, stages):
    depth = len(stages)
    group = next(g for g in ITEMS_PER_STAGE if items % g == 0)
    count = items // group

    def tick(t, static):
        for s in reversed(range(depth)):
            if not static or 0 <= t - s < count:
                for u in range(group):
                    stages[s]((t - s) * group + u)

    if count < depth:
        for t in range(count + depth - 1):
            tick(t, True)
        return
    for t in range(depth - 1):
        tick(t, True)

    def body(t, carry):
        tick(t, False)
        return carry
    lax.fori_loop(depth - 1, count, body, 0)
    for t in range(count, count + depth - 1):
        tick(t, True)

def _attn_prompt_kernel(q_ref, k_ref, v_ref, tab_ref, out_ref, obuf, lbuf, lg_s, p_s, s_s, l_s,
                        *, seq):
    col = lax.broadcasted_iota(jnp.int32, (CHUNK, LANE), 1)
    lane_lo = col < HEAD_DIM

    for bi, (win, dil) in enumerate(DIL_BRANCHES):
        assert win // dil == CHUNK
        nfirst = dil
        nrest = seq // CHUNK - dil
        assert 2 * (nfirst + nrest) <= lg_s.shape[0]

        def rows(st, dil=dil):
            return pl.ds(st, CHUNK) if dil == 1 else pl.ds(st, CHUNK, stride=dil)

        def place(i, with_prev, dil=dil, nfirst=nfirst):
            if not with_prev:
                return i, 2 * i, CHUNK
            start = i % dil + dil * CHUNK * (1 + i // dil)
            if dil == 1 and not isinstance(start, int):
                start = pl.multiple_of(start, CHUNK)
            return start, 2 * (nfirst + i), 2 * CHUNK

        def window(ref, start, with_prev, dil=dil, rows=rows):
            w = ref[rows(start), :]
            if with_prev:
                w = jnp.concatenate([ref[rows(start - dil * CHUNK), :], w], axis=0)
            return w.astype(BF16)

        def logits(i, with_prev, bi=bi, rows=rows, place=place, window=window):
            start, slot, width = place(i, with_prev)
            qb = q_ref[rows(start), :]
            kw = window(k_ref, start, with_prev)
            for hh in range(2):
                sel = lane_lo if hh == 0 else jnp.logical_not(lane_lo)
                qm = jnp.where(sel, qb, 0.0).astype(BF16)
                lg_s[slot + hh, :, 0:width] = (_dot_nt(qm, kw)
                                               + tab_ref[bi, hh, :, 2 * CHUNK - width:])

        def rowmax(i, with_prev, place=place):
            _, slot, width = place(i, with_prev)
            for j in (slot, slot + 1):
                mx = jnp.max(lg_s[j, :, 0:width], axis=1, keepdims=True)
                l_s[j] = jnp.broadcast_to(mx, (CHUNK, LANE))

        def softmax(i, with_prev, place=place):
            _, slot, width = place(i, with_prev)
            for j in (slot, slot + 1):
                mx = l_s[j]
                pr = jnp.concatenate([jnp.exp(lg_s[j, :, k:k + LANE] - mx)
                                      for k in range(0, width, LANE)], axis=1)
                sm = jnp.broadcast_to(jnp.sum(pr, axis=1, keepdims=True), (CHUNK, LANE))
                p_s[j, :, 0:width] = pr.astype(BF16)
                s_s[j] = sm
                l_s[j] = mx + jnp.log(sm)

        def weighted(i, with_prev, bi=bi, rows=rows, place=place, window=window):
            start, slot, width = place(i, with_prev)
            vw = window(v_ref, start, with_prev)
            pv0 = jnp.dot(p_s[slot, :, 0:width], vw, preferred_element_type=F32)
            pv1 = jnp.dot(p_s[slot + 1, :, 0:width], vw, preferred_element_type=F32)
            obuf[bi, rows(start), :] = jnp.where(lane_lo, pv0 / s_s[slot], pv1 / s_s[slot + 1])
            lbuf[bi, rows(start), :] = jnp.where(lane_lo, l_s[slot], l_s[slot + 1])

        for with_prev, count in ((False, nfirst), (True, nrest)):
            _pipelined_loop(count, [functools.partial(stage, with_prev=with_prev)
                                    for stage in (logits, rowmax, softmax, weighted)])

    step = 2 * CHUNK

    def combine(i, carry):
        r = pl.ds(pl.multiple_of(i * step, step), step)
        l0, l1, l2 = lbuf[0, r, :], lbuf[1, r, :], lbuf[2, r, :]
        mx = jnp.maximum(jnp.maximum(l0, l1), l2)
        e0, e1, e2 = jnp.exp(l0 - mx), jnp.exp(l1 - mx), jnp.exp(l2 - mx)
        den = e0 + e1 + e2
        out_ref[r, :] = ((e0 / den) * obuf[0, r, :] + (e1 / den) * obuf[1, r, :]
                         + (e2 / den) * obuf[2, r, :])
        return carry
    lax.fori_loop(0, seq // step, combine, 0)


def _attn_prompt(z, tab, *, nseq, seq):
    n = z.shape[0]
    npair = H_A // 2
    nb = len(DIL_BRANCHES)
    nslot = 2 * (seq // CHUNK)
    kern = functools.partial(_attn_prompt_kernel, seq=seq)
    return pl.pallas_call(
        kern,
        grid=(nseq, npair),
        in_specs=[pl.BlockSpec((seq, LANE), lambda b, p: (b, Z_AQ // LANE + p)),
                  pl.BlockSpec((seq, LANE), lambda b, p: (b, Z_AK // LANE + p)),
                  pl.BlockSpec((seq, LANE), lambda b, p: (b, Z_AV // LANE + p)),
                  pl.BlockSpec((nb, 2, CHUNK, 2 * CHUNK), lambda b, p: (0, p, 0, 0))],
        out_specs=pl.BlockSpec((seq, LANE), lambda b, p: (b, p)),
        out_shape=jax.ShapeDtypeStruct((n, W_A), F32),
        scratch_shapes=[pltpu.VMEM((nb, seq, LANE), F32),
                        pltpu.VMEM((nb, seq, LANE), F32),
                        pltpu.VMEM((nslot, CHUNK, 2 * CHUNK), F32),
                        pltpu.VMEM((nslot, CHUNK, 2 * CHUNK), BF16),
                        pltpu.VMEM((nslot, CHUNK, LANE), F32),
                        pltpu.VMEM((nslot, CHUNK, LANE), F32)],
        compiler_params=_params("parallel", "parallel"),
        name="attn_prompt",
    )(z, z, z, tab)


def _prompt_bias_table(rel_bias):
    qi = np.arange(CHUNK)[:, None]
    kj = np.arange(2 * CHUNK)[None, :]
    delta = qi + CHUNK - kj
    tabs = []
    for win, dil in DIL_BRANCHES:
        wc = win // dil
        valid = (delta >= 0) & (delta <= wc)
        bias = _bias_lookup(rel_bias, _rel_bucket(np.clip(delta, 0, None) * dil))
        tabs.append(jnp.where(valid[None], bias, NEG))
    return jnp.stack(tabs)


def _bias_lookup(rel_bias, bucket):
    onehot = (jnp.asarray(bucket)[..., None] == jnp.arange(N_BUCKETS)).astype(F32)
    return jnp.einsum("...n,nh->h...", onehot, rel_bias.astype(F32),
                      precision=lax.Precision.HIGHEST)


def _bdot(a, b, contract):
    return lax.dot_general(a, b, ((contract[0], contract[1]), ((0,), (0,))),
                           preferred_element_type=F32)


def _attn_sample_kernel(li_ref, q_ref, kn_ref, vn_ref, kt_ref, vt_ref, *rest, tokens, wbuf):
    del li_ref
    nb = len(DIL_BRANCHES)
    tab_refs, tabn_ref, out_ref = rest[:nb], rest[nb], rest[nb + 1]
    q = q_ref[0].astype(BF16)
    kt = kt_ref[0, 0].astype(BF16)
    vt = vt_ref[0, 0].astype(BF16)
    pad = jnp.zeros((H_A, LANE - tokens, HEAD_DIM), F32)
    kn = jnp.concatenate([kn_ref[0], pad], axis=1).astype(BF16)
    vn = jnp.concatenate([vn_ref[0], pad], axis=1).astype(BF16)
    lg_cache = _bdot(q, kt, ((2,), (1,)))
    lg_new = _bdot(q, kn, ((2,), (2,)))

    probs, probs_new, stats = [], [], []
    for bi, (win, dil) in enumerate(DIL_BRANCHES):
        lo = wbuf - win
        lg = lg_cache[:, :, lo:] + tab_refs[bi][...]
        ln = lg_new + tabn_ref[bi]
        mx = jnp.maximum(jnp.max(lg, axis=-1, keepdims=True), jnp.max(ln, axis=-1, keepdims=True))
        pr = jnp.exp(lg - mx)
        pn = jnp.exp(ln - mx)
        sm = jnp.sum(pr, axis=-1, keepdims=True) + jnp.sum(pn, axis=-1, keepdims=True)
        if lo > 0:
            pr = jnp.concatenate([jnp.zeros((H_A, tokens, lo), F32), pr], axis=-1)
        probs.append(pr)
        probs_new.append(pn)
        stats.append((mx, sm))
    p_all = jnp.concatenate(probs, axis=1).astype(BF16)
    pn_all = jnp.concatenate(probs_new, axis=1).astype(BF16)
    o_all = _bdot(p_all, vt, ((2,), (2,))) + _bdot(pn_all, vn, ((2,), (1,)))

    lses = [mx + jnp.log(sm) for mx, sm in stats]
    top = functools.reduce(jnp.maximum, lses)
    es = [jnp.exp(l - top) for l in lses]
    den = functools.reduce(lambda a, b: a + b, es)
    out = None
    for bi in range(nb):
        term = (es[bi] / den) * (o_all[:, bi * tokens:(bi + 1) * tokens] / stats[bi][1])
        out = term if out is None else out + term
    out_ref[0] = out


def _attn_sample(layer, q, kn, vn, cache_kt, cache_vt, tabs, tabn):
    nseq, _, tokens, _ = q.shape
    wbuf = cache_kt.shape[-1]
    new_spec = pl.BlockSpec((1, H_A, tokens, HEAD_DIM), lambda b, li: (b, 0, 0, 0))
    cache_spec = pl.BlockSpec((1, 1, H_A, HEAD_DIM, wbuf), lambda b, li: (li[0], b, 0, 0, 0))
    tab_specs = [pl.BlockSpec(t.shape, lambda b, li: (0, 0, 0)) for t in tabs]
    kern = functools.partial(_attn_sample_kernel, tokens=tokens, wbuf=wbuf)
    return pl.pallas_call(
        kern,
        grid_spec=pltpu.PrefetchScalarGridSpec(
            num_scalar_prefetch=1,
            grid=(nseq,),
            in_specs=[new_spec, new_spec, new_spec, cache_spec, cache_spec, *tab_specs,
                      pl.BlockSpec(tabn.shape, lambda b, li: (0, 0, 0, 0))],
            out_specs=new_spec),
        out_shape=jax.ShapeDtypeStruct((nseq, H_A, tokens, HEAD_DIM), F32),
        compiler_params=_params("parallel"),
        name="attn_sample",
    )(layer, q, kn, vn, cache_kt, cache_vt, *tabs, tabn)


def _sample_bias_tables(rel_bias, tokens, wbuf):
    assert wbuf >= max(w for w, _ in DIL_BRANCHES)
    tok = np.arange(tokens)[:, None]
    tabs = []
    for win, dil in DIL_BRANCHES:
        pos = np.arange(wbuf - win, wbuf)[None, :]
        dist = wbuf + tok - pos
        valid = (dist % dil == 0) & (dist <= win)
        tabs.append(jnp.where(valid[None], _bias_lookup(rel_bias, _rel_bucket(dist)), NEG))
    other = np.arange(LANE)[None, :]
    dist = tok - other
    news = []
    for win, dil in DIL_BRANCHES:
        valid = (dist >= 0) & (dist % dil == 0) & (dist <= win)
        news.append(jnp.where(valid[None], _bias_lookup(rel_bias, _rel_bucket(np.clip(dist, 0, None))),
                              NEG))
    return tabs, jnp.stack(news)


def _post_kernel(x_ref, hm_ref, ha_ref, pe_ref, wo_ref, g2_ref, wg_ref, wu_ref, wd_ref, g3_ref,
                 wpg_ref, wpe_ref, out_ref, x1_s, h2_s, acc, *, nff):
    j = pl.program_id(1)

    @pl.when(j == 0)
    def _mix():
        x1 = (x_ref[...]
              + jnp.dot(hm_ref[...].astype(BF16), wo_ref[0:W_M, :], preferred_element_type=F32)
              + jnp.dot(ha_ref[...].astype(BF16), wo_ref[W_M:, :], preferred_element_type=F32))
        x1_s[...] = x1
        ms = jnp.mean(x1 * x1, axis=-1, keepdims=True)
        h2_s[...] = (x1 * lax.rsqrt(ms + EPS) * g2_ref[...]).astype(BF16)
        acc[...] = jnp.zeros(acc.shape, F32)

    h2 = h2_s[...]
    gate = jnp.dot(h2, wg_ref[...], preferred_element_type=F32)
    up = jnp.dot(h2, wu_ref[...], preferred_element_type=F32)
    act = (gate * jax.nn.sigmoid(gate) * up).astype(BF16)
    acc[...] += jnp.dot(act, wd_ref[...], preferred_element_type=F32)

    @pl.when(j == nff - 1)
    def _fin():
        x2 = x1_s[...] + acc[...]
        ms = jnp.mean(x2 * x2, axis=-1, keepdims=True)
        h3 = (x2 * lax.rsqrt(ms + EPS) * g3_ref[...]).astype(BF16)
        pg = jax.nn.sigmoid(jnp.dot(h3, wpg_ref[...], preferred_element_type=F32))
        pp = jnp.dot(pe_ref[...].astype(BF16), wpe_ref[...], preferred_element_type=F32)
        out_ref[...] = x2 + pg * pp


def _post(x, hm, ha, pe, wo, g2, wgu, wd, g3, wpg, wpe):
    n = x.shape[0]
    tm = min(ROWS_POST, n)
    nff = D_FF // FF_BLOCK
    row = lambda i, j: (i, 0)
    fixed = lambda shape: pl.BlockSpec(shape, lambda i, j: (0, 0), pipeline_mode=pl.Buffered(1))
    kern = functools.partial(_post_kernel, nff=nff)
    return pl.pallas_call(
        kern,
        grid=(n // tm, nff),
        in_specs=[pl.BlockSpec((tm, D_MODEL), row),
                  pl.BlockSpec((tm, W_M), row),
                  pl.BlockSpec((tm, W_A), row),
                  pl.BlockSpec((tm, D_PLE), row),
                  fixed((D_MODEL, D_MODEL)),
                  fixed((1, D_MODEL)),
                  pl.BlockSpec((D_MODEL, FF_BLOCK), lambda i, j: (0, j)),
                  pl.BlockSpec((D_MODEL, FF_BLOCK), lambda i, j: (0, nff + j)),
                  pl.BlockSpec((FF_BLOCK, D_MODEL), lambda i, j: (j, 0)),
                  fixed((1, D_MODEL)),
                  fixed((D_MODEL, D_MODEL)),
                  fixed((D_PLE, D_MODEL))],
        out_specs=pl.BlockSpec((tm, D_MODEL), row),
        out_shape=jax.ShapeDtypeStruct((n, D_MODEL), F32),
        scratch_shapes=[pltpu.VMEM((tm, D_MODEL), F32),
                        pltpu.VMEM((tm, D_MODEL), BF16),
                        pltpu.VMEM((tm, D_MODEL), F32)],
        compiler_params=_params("parallel", "arbitrary"),
        name="post",
    )(x, hm, ha, pe, wo, g2, wgu, wgu, wd, g3, wpg, wpe)


def _mixer_tail(x, hm, ha, pe, lw):
    return _post(x, hm, ha, pe, lw["w_out"], lw["norm2_g"], lw["w_gu"], lw["w_down"],
                 lw["ple_norm_g"], lw["w_pg"], lw["w_pe"])


def _layer(xp, xs, layer, filled, lw, *, depth, batch, seq, dec_batch, dec_seq, cache_k, cache_v,
           state_c, tab_p, tab_s, tab_n, seg):
    def proj(x, win=None):
        return _in_proj(x, lw["norm1_g"], lw["w_in"], lw["q_norm_g"], lw["k_norm_g"], seg, win)

    kt, vt, c_new = filled
    zp, kt, vt = proj(xp, (layer, depth, batch, seq, kt, vt))
    hm, pc, pn, pm = _mlstm_prompt(zp, lw["conv_w"], lw["conv_b"], lw["b_i"], lw["b_f"],
                                   lw["mh_norm_g"], nseq=batch, nchunks=seq // CHUNK)
    ha = _attn_prompt(zp, tab_p, nseq=batch, seq=seq)
    xp_new = _mixer_tail(xp, hm, ha, lw["p_prompt"], lw)
    zp3 = zp.reshape(batch, seq, Z_W)
    p_out = (_from_blockdiag(pc),
             pn.reshape(batch, H_M, HEAD_DIM),
             pm[:, 0, :H_M],
             zp3[:, seq - (CONV_W - 1):, Z_QK:Z_QK + 2 * W_M])

    zs, = proj(xs)
    zs3 = zs.reshape(dec_batch, dec_seq, Z_W)
    lanes = lambda a: jnp.broadcast_to(a[..., None], a.shape + (dec_batch,))
    gates = jnp.stack([zs3[:, :, Z_GI:Z_GI + H_M], zs3[:, :, Z_GF:Z_GF + H_M]], axis=2)
    hm_t, c_new, sn, sm = _mlstm_sample(
        jnp.transpose(zs3[:, :, :Z_AQ], (1, 2, 0)),
        jnp.transpose(gates, (3, 1, 2, 0)),
        jnp.transpose(lw["state_conv"], (1, 2, 0)),
        lanes(lw["conv_w"]), lanes(lw["conv_b"][0]),
        lanes(jnp.stack([lw["b_i"][0, :H_M], lw["b_f"][0, :H_M]], axis=1)), lanes(lw["mh_norm_g"][0]),
        layer, state_c, c_new,
        jnp.transpose(lw["state_n"], (1, 2, 0)),
        jnp.transpose(lw["state_m"], (1, 0))[:, None, :])
    hm = jnp.transpose(hm_t, (2, 0, 1)).reshape(dec_batch * dec_seq, W_M)
    heads = (dec_batch, dec_seq, H_A, HEAD_DIM)
    q_s = zs3[:, :, Z_AQ:Z_AQ + W_A].reshape(heads)
    k_s = zs3[:, :, Z_AK:Z_AK + W_A].reshape(heads)
    v_s = zs3[:, :, Z_AV:Z_AV + W_A].reshape(heads)
    by_head = lambda t: jnp.transpose(t, (0, 2, 1, 3))
    ha = _attn_sample(lw["layer"], by_head(q_s), by_head(k_s), by_head(v_s), cache_k, cache_v,
                      tab_s, tab_n)
    ha = by_head(ha).reshape(dec_batch * dec_seq, W_A)
    xs_new = _mixer_tail(xs, hm, ha, lw["p_sample"], lw)
    conv_rows = jnp.concatenate([lw["state_conv"], zs3[:, :, Z_QK:Z_QK + 2 * W_M]], axis=1)
    s_out = (k_s, v_s,
             jnp.transpose(sn, (2, 0, 1)),
             jnp.transpose(sm[:, 0, :], (1, 0)),
             conv_rows[:, -(CONV_W - 1):])
    return xp_new, xs_new, (kt, vt, c_new), p_out, s_out


def kernel(x_prompt, x_sample, p_prompt, p_sample, cache_attn_k, cache_attn_v, state_mlstm_C, state_mlstm_n, state_mlstm_m, state_conv, rel_bias, norm1_g, w_in, b_if, conv_w, conv_b, mh_norm_g, q_norm_g, k_norm_g, w_out, norm2_g, w_gu, w_down, ple_norm_g, w_pe, w_pg):
    batch, seq, _ = x_prompt.shape
    dec_batch, dec_seq, _ = x_sample.shape
    depth = w_in.shape[0]
    assert seq % (CHUNK * max(d for _, d in DIL_BRANCHES)) == 0
    assert seq <= max(w for w, _ in DIL_BRANCHES)

    n_m = 4 * W_M
    gate_pad = jnp.zeros((depth, D_MODEL, LANE - H_M), w_in.dtype)
    w_in_z = jnp.concatenate(
        [w_in[:, :, :n_m], w_in[:, :, n_m + 2 * H_M:],
         w_in[:, :, n_m:n_m + H_M], gate_pad, w_in[:, :, n_m + H_M:n_m + 2 * H_M], gate_pad],
        axis=-1).astype(BF16)
    b_gate = lambda lo: jnp.pad(b_if[:, lo:lo + H_M], ((0, 0), (0, LANE - H_M)))[:, None, :]

    head_of_lane = jnp.arange(W_A) // HEAD_DIM
    seg = (head_of_lane[:, None] == head_of_lane[None, :]).astype(BF16)
    tab_p = _prompt_bias_table(rel_bias)
    tab_s, tab_n = _sample_bias_tables(rel_bias, dec_seq, cache_attn_k.shape[2])
    cache_kt = jnp.transpose(cache_attn_k, (0, 1, 3, 4, 2))
    cache_vt = jnp.transpose(cache_attn_v, (0, 1, 3, 4, 2))

    row = lambda a: a[:, None, :]
    per_layer = dict(
        layer=jnp.arange(depth, dtype=jnp.int32)[:, None],
        norm1_g=row(norm1_g), w_in=w_in_z,
        q_norm_g=row(jnp.tile(q_norm_g, (1, H_A))), k_norm_g=row(jnp.tile(k_norm_g, (1, H_A))),
        conv_w=conv_w, conv_b=row(conv_b),
        b_i=b_gate(0), b_f=b_gate(H_M),
        mh_norm_g=row(mh_norm_g),
        w_out=w_out.astype(BF16), norm2_g=row(norm2_g),
        w_gu=w_gu.astype(BF16), w_down=w_down.astype(BF16),
        ple_norm_g=row(ple_norm_g), w_pg=w_pg.astype(BF16), w_pe=w_pe.astype(BF16),
        p_prompt=p_prompt.reshape(depth, batch * seq, D_PLE),
        p_sample=p_sample.reshape(depth, dec_batch * dec_seq, D_PLE),
        state_conv=state_conv, state_n=state_mlstm_n, state_m=state_mlstm_m,
    )
    xp = x_prompt.reshape(batch * seq, D_MODEL)
    xs = x_sample.reshape(dec_batch * dec_seq, D_MODEL)
    p_outs, s_outs = [], []
    filled = (None, None, None)
    state_c = jnp.transpose(state_mlstm_C, (0, 2, 3, 4, 1))
    for i in range(depth):
        lw = {name: val[i] for name, val in per_layer.items()}
        xp, xs, filled, p_out, s_out = _layer(
            xp, xs, i, filled, lw, depth=depth, batch=batch, seq=seq, dec_batch=dec_batch,
            dec_seq=dec_seq, cache_k=cache_kt, cache_v=cache_vt, state_c=state_c,
            tab_p=tab_p, tab_s=tab_s, tab_n=tab_n, seg=seg)
        p_outs.append(p_out)
        s_outs.append(s_out)
    stack = lambda outs: tuple(jnp.stack(leaf) for leaf in zip(*outs))
    kt, vt, c_new = filled
    window = lambda t: jnp.transpose(t.reshape(depth, batch, H_A, HEAD_DIM, seq), (0, 1, 4, 2, 3))
    s_k, s_v, *s_rest = stack(s_outs)
    return (xp.reshape(batch, seq, D_MODEL), xs.reshape(dec_batch, dec_seq, D_MODEL),
            window(kt), window(vt), *stack(p_outs),
            s_k, s_v, jnp.transpose(c_new, (0, 4, 1, 2, 3)), *s_rest)
```

```python
import functools
import math

import jax
import jax.numpy as jnp
import numpy as np
from jax import lax
from jax.experimental import pallas as pl
from jax.experimental.pallas import tpu as pltpu

F32 = jnp.float32
BF16 = jnp.bfloat16

D_MODEL = 1024
HEAD_DIM = 64
H_M = 8
H_A = 8
W_M = H_M * HEAD_DIM
W_A = H_A * HEAD_DIM
CONV_W = 4
CHUNK = 128
DIL_BRANCHES = ((128, 1), (512, 4), (2048, 16))
N_BUCKETS = 32
MAX_DIST = 2048
D_FF = 2816
D_PLE = 256
EPS = 1e-6
NEG = -1e30

LANE = 128
SUBLANE = 8
VMEM_LIMIT_BYTES = 56 * 1024 * 1024

Z_QK = 0
Z_V = 2 * W_M
Z_O = 3 * W_M
Z_AQ = 4 * W_M
Z_AK = Z_AQ + W_A
Z_AV = Z_AK + W_A
Z_GI = Z_AV + W_A
Z_GF = Z_GI + LANE
Z_W = Z_GF + LANE

ROWS_IN = 256
ROWS_POST = 512
FF_BLOCK = D_FF // 2


def _params(*sem):
    return pltpu.CompilerParams(dimension_semantics=sem, vmem_limit_bytes=VMEM_LIMIT_BYTES)


def _rel_bucket(dist):
    max_exact = N_BUCKETS // 2
    df = np.maximum(dist, 1).astype(np.float32)
    scaled = (np.log(df / np.float32(max_exact)) / np.float32(math.log(MAX_DIST / max_exact))
              * np.float32(N_BUCKETS - max_exact))
    large = np.minimum(max_exact + scaled.astype(np.int32), N_BUCKETS - 1)
    return np.where(dist < max_exact, dist, large).astype(np.int32)


def _split_dot(p, ones_bf16):
    hi = p.astype(BF16)
    lo = (p - hi.astype(F32)).astype(BF16)
    return (jnp.dot(hi, ones_bf16, preferred_element_type=F32)
            + jnp.dot(lo, ones_bf16, preferred_element_type=F32))


def _dot_nt(a, b):
    return lax.dot_general(a, b, (((1,), (1,)), ((), ())), preferred_element_type=F32)


def _in_proj_kernel(x_ref, g_ref, w_ref, qg_ref, kg_ref, seg_ref, *rest, windows):
    outs = rest[len(rest) - (3 if windows else 1):]
    z_ref = outs[0]
    x = x_ref[...]
    ms = jnp.mean(x * x, axis=-1, keepdims=True)
    h = (x * lax.rsqrt(ms + EPS) * g_ref[...]).astype(BF16)
    z_ref[:, 0:Z_AQ] = jnp.dot(h, w_ref[:, 0:Z_AQ], preferred_element_type=F32)
    for off, gr, mult in ((Z_AQ, qg_ref, HEAD_DIM ** -0.5), (Z_AK, kg_ref, None)):
        a = jnp.dot(h, w_ref[:, off:off + W_A], preferred_element_type=F32)
        ss = _split_dot(a * a, seg_ref[...])
        a = a * lax.rsqrt(ss * (1.0 / HEAD_DIM) + EPS) * gr[...]
        z_ref[:, off:off + W_A] = a if mult is None else a * mult
        if windows and off == Z_AK:
            outs[1][...] = a.T
    tail = jnp.dot(h, w_ref[:, Z_AV:Z_W], preferred_element_type=F32)
    z_ref[:, Z_AV:Z_W] = tail
    if windows:
        outs[2][...] = tail[:, 0:W_A].T


def _in_proj(x, g, w, qg, kg, seg, windows=None):
    n = x.shape[0]
    tm = min(ROWS_IN, n)
    const = lambda i: (0, 0)
    in_specs = [pl.BlockSpec((tm, D_MODEL), lambda i: (i, 0)),
                pl.BlockSpec((1, D_MODEL), const),
                pl.BlockSpec((D_MODEL, Z_W), const),
                pl.BlockSpec((1, W_A), const),
                pl.BlockSpec((1, W_A), const),
                pl.BlockSpec((W_A, W_A), const)]
    out_specs = [pl.BlockSpec((tm, Z_W), lambda i: (i, 0))]
    out_shape = [jax.ShapeDtypeStruct((n, Z_W), F32)]
    args = [x, g, w, qg, kg, seg]
    aliases = {}
    if windows is not None:
        layer, depth, nseq, seq, kt, vt = windows
        per_seq = seq // tm
        win_spec = pl.BlockSpec((None, None, W_A, tm),
                                lambda i: (layer, i // per_seq, 0, i % per_seq))
        out_specs += [win_spec, win_spec]
        out_shape += [jax.ShapeDtypeStruct((depth, nseq, W_A, seq), F32)] * 2
        if kt is not None:
            aliases = {len(args): 1, len(args) + 1: 2}
            in_specs += [pl.BlockSpec(memory_space=pl.ANY)] * 2
            args += [kt, vt]
    return pl.pallas_call(
        functools.partial(_in_proj_kernel, windows=windows is not None),
        grid=(n // tm,),
        in_specs=in_specs,
        out_specs=out_specs,
        out_shape=out_shape,
        input_output_aliases=aliases,
        compiler_params=_params("parallel"),
        name="in_proj",
    )(*args)


def _log_sigmoid(x):
    return jnp.minimum(x, 0.0) - jnp.log1p(jnp.exp(-jnp.abs(x)))


def _scan_rows(x, combine, identity):
    row = lax.broadcasted_iota(jnp.int32, x.shape, 0)
    k = 1
    while k < x.shape[0]:
        x = combine(x, jnp.where(row >= k, pltpu.roll(x, k, axis=0), identity))
        k *= 2
    return x


def _bf16_parts(x, count):
    parts = []
    for _ in range(count):
        part = x.astype(BF16)
        parts.append(part)
        x = x - part.astype(F32)
    return parts


def _expand(parts, ones_bf16):
    out = None
    for part in parts:
        term = jnp.dot(part, ones_bf16, preferred_element_type=F32)
        out = term if out is None else out + term
    return out


def _mlstm_prompt_kernel(u_ref, v_ref, o_ref, gi_ref, gf_ref, cw_ref, cb_ref, bi_ref, bf_ref, mhg_ref,
                         exh_ref, exp_ref, ones_ref, seg_ref,
                         hm_ref, cout_ref, nout_ref, mout_ref,
                         ubuf, cbd, nst, mst, *, nchunks):
    c = pl.program_id(1)

    @pl.when(c == 0)
    def _init():
        ubuf[0:SUBLANE, :] = jnp.zeros((SUBLANE, 2 * W_M), F32)
        cbd[...] = jnp.zeros(cbd.shape, F32)
        nst[...] = jnp.zeros(nst.shape, F32)
        mst[...] = jnp.zeros(mst.shape, F32)

    ubuf[SUBLANE:SUBLANE + CHUNK, :] = u_ref[...]

    def conv_silu(lo):
        y = cb_ref[:, lo:lo + LANE]
        for j in range(CONV_W):
            off = SUBLANE - (CONV_W - 1) + j
            y = y + ubuf[off:off + CHUNK, lo:lo + LANE] * cw_ref[j:j + 1, lo:lo + LANE]
        return y * jax.nn.sigmoid(y)

    row = lax.broadcasted_iota(jnp.int32, (CHUNK, LANE), 0)
    col = lax.broadcasted_iota(jnp.int32, (CHUNK, LANE), 1)
    lane_lo = col < HEAD_DIM
    causal = row >= col
    same_head = (row < HEAD_DIM) == lane_lo
    last = slice(CHUNK - 1, CHUNK)

    b = _scan_rows(_log_sigmoid(gf_ref[...] + bf_ref[...]), jnp.add, 0.0)
    g = gi_ref[...] + bi_ref[...] - b
    m_prev = mst[...]
    big_m = jnp.maximum(m_prev, _scan_rows(g, jnp.maximum, -jnp.inf))
    m_t = b + big_m
    mst[...] = m_t[last, :]
    g_t = g.T
    big_m_parts = _bf16_parts(big_m, 3)
    inter_parts = _bf16_parts(jnp.exp(m_prev - big_m), 2)
    floor_parts = _bf16_parts(jnp.exp(-m_t), 2)
    w_end_parts = _bf16_parts(jnp.exp(g - big_m[last, :]), 2)

    for p in range(H_M // 2):
        sl = slice(p * LANE, (p + 1) * LANE)
        big_m_2 = _expand(big_m_parts, exh_ref[p])
        big_m_h = (big_m_2[:, 0:LANE], big_m_2[:, LANE:])
        inter_p = _expand(inter_parts, exp_ref[p])
        floor_p = _expand(floor_parts, exp_ref[p])
        w_p = _expand(w_end_parts, exp_ref[p])

        q_p = conv_silu(p * LANE)
        k_p = conv_silu(W_M + p * LANE) * HEAD_DIM ** -0.5
        q_bf = q_p.astype(BF16)
        k_bf = k_p.astype(BF16)
        v_p = v_ref[:, sl]
        n_p = nst[:, sl]
        c_old = cbd[p]

        s_heads = []
        for hh in range(2):
            h = 2 * p + hh
            sel = lane_lo if hh == 0 else jnp.logical_not(lane_lo)
            qm = jnp.where(sel, q_p, 0.0).astype(BF16)
            decay_mat = jnp.exp(jnp.where(causal, g_t[h:h + 1, :] - big_m_h[hh], NEG))
            s_heads.append((_dot_nt(qm, k_bf) * decay_mat).astype(BF16))
        s_cat = jnp.concatenate(s_heads, axis=1)
        v_cat = jnp.concatenate([jnp.where(lane_lo, v_p, 0.0), jnp.where(lane_lo, 0.0, v_p)], axis=0)
        intra = jnp.dot(s_cat, jnp.concatenate([v_cat.astype(BF16), ones_ref[...]], axis=1),
                        preferred_element_type=F32)
        n_mat = jnp.where(same_head, jnp.broadcast_to(n_p, (LANE, LANE)), 0.0)
        carried = _dot_nt(q_bf, jnp.concatenate([c_old, n_mat], axis=0).astype(BF16))
        num = inter_p * carried[:, 0:LANE] + intra[:, 0:LANE]
        den = inter_p * carried[:, LANE:] + intra[:, LANE:]
        hv = num / jnp.maximum(jnp.abs(den), floor_p)
        hv = hv * jax.nn.sigmoid(o_ref[:, sl])
        ms = _split_dot(hv * hv, seg_ref[...]) * (1.0 / HEAD_DIM)
        hm_ref[:, sl] = hv * lax.rsqrt(ms + EPS) * mhg_ref[:, sl]

        dec_p = inter_p[last, :]
        c_upd = jnp.dot((v_p * w_p).T.astype(BF16), k_bf, preferred_element_type=F32)
        cbd[p] = dec_p * c_old + jnp.where(same_head, c_upd, 0.0)
        nst[:, sl] = dec_p * n_p + jnp.sum(k_p * w_p, axis=0, keepdims=True)
    ubuf[0:SUBLANE, :] = ubuf[CHUNK:CHUNK + SUBLANE, :]

    @pl.when(c == nchunks - 1)
    def _fin():
        cout_ref[0] = cbd[...]
        nout_ref[0] = nst[...]
        mout_ref[0] = mst[...]


def _mlstm_constants():
    npair = H_M // 2
    half = np.arange(LANE) // HEAD_DIM
    per_head = np.zeros((npair, LANE, 2 * LANE), np.float32)
    per_pair = np.zeros((npair, LANE, LANE), np.float32)
    for p in range(npair):
        per_head[p, 2 * p, 0:LANE] = 1.0
        per_head[p, 2 * p + 1, LANE:] = 1.0
        per_pair[p, 2 * p + half, np.arange(LANE)] = 1.0
    ones = np.zeros((2 * CHUNK, LANE), np.float32)
    ones[:CHUNK, :HEAD_DIM] = 1.0
    ones[CHUNK:, HEAD_DIM:] = 1.0
    seg = (half[:, None] == half[None, :]).astype(np.float32)
    return tuple(jnp.asarray(a, BF16) for a in (per_head, per_pair, ones, seg))


def _mlstm_prompt(z, cw, cb, b_i, b_f, mhg, *, nseq, nchunks):
    n = z.shape[0]
    tok = lambda b, c: b * nchunks + c
    const2 = lambda b, c: (0, 0)
    const3 = lambda b, c: (0, 0, 0)
    kern = functools.partial(_mlstm_prompt_kernel, nchunks=nchunks)
    npair = H_M // 2
    per_head, per_pair, ones, seg = _mlstm_constants()
    return pl.pallas_call(
        kern,
        grid=(nseq, nchunks),
        in_specs=[pl.BlockSpec((CHUNK, 2 * W_M), lambda b, c: (tok(b, c), Z_QK // (2 * W_M))),
                  pl.BlockSpec((CHUNK, W_M), lambda b, c: (tok(b, c), Z_V // W_M)),
                  pl.BlockSpec((CHUNK, W_M), lambda b, c: (tok(b, c), Z_O // W_M)),
                  pl.BlockSpec((CHUNK, LANE), lambda b, c: (tok(b, c), Z_GI // LANE)),
                  pl.BlockSpec((CHUNK, LANE), lambda b, c: (tok(b, c), Z_GF // LANE)),
                  pl.BlockSpec((CONV_W, 2 * W_M), const2),
                  pl.BlockSpec((1, 2 * W_M), const2),
                  pl.BlockSpec((1, LANE), const2),
                  pl.BlockSpec((1, LANE), const2),
                  pl.BlockSpec((1, W_M), const2),
                  pl.BlockSpec(per_head.shape, const3),
                  pl.BlockSpec(per_pair.shape, const3),
                  pl.BlockSpec(ones.shape, const2),
                  pl.BlockSpec(seg.shape, const2)],
        out_specs=[pl.BlockSpec((CHUNK, W_M), lambda b, c: (tok(b, c), 0)),
                   pl.BlockSpec((1, npair, LANE, LANE), lambda b, c: (b, 0, 0, 0)),
                   pl.BlockSpec((1, 1, W_M), lambda b, c: (b, 0, 0)),
                   pl.BlockSpec((1, 1, LANE), lambda b, c: (b, 0, 0))],
        out_shape=[jax.ShapeDtypeStruct((n, W_M), F32),
                   jax.ShapeDtypeStruct((nseq, npair, LANE, LANE), F32),
                   jax.ShapeDtypeStruct((nseq, 1, W_M), F32),
                   jax.ShapeDtypeStruct((nseq, 1, LANE), F32)],
        scratch_shapes=[pltpu.VMEM((CHUNK + SUBLANE, 2 * W_M), F32),
                        pltpu.VMEM((npair, LANE, LANE), F32),
                        pltpu.VMEM((1, W_M), F32),
                        pltpu.VMEM((1, LANE), F32)],
        compiler_params=_params("parallel", "arbitrary"),
        name="mlstm_prompt",
    )(z, z, z, z, z, cw, cb, b_i, b_f, mhg, per_head, per_pair, ones, seg)


def _mlstm_sample_kernel(uq_ref, uk_ref, v_ref, o_ref, g_ref, pq_ref, pk_ref,
                         cwq_ref, cwk_ref, cbq_ref, cbk_ref, bif_ref, mhg_ref,
                         c_ref, n_ref, m_ref, *rest, tokens):
    hm_ref, cout_ref, nout_ref, mout_ref, q_s, k_s, vw_s, qc_s = rest[len(rest) - 8:]
    def conv(u_ref, p_ref, cw_ref, cb_ref):
        rows = [p_ref[j] for j in range(CONV_W - 1)] + [u_ref[t] for t in range(tokens)]
        out = []
        for t in range(tokens):
            y = cb_ref[...]
            for j in range(CONV_W):
                y = y + rows[t + j] * cw_ref[j]
            out.append(y * jax.nn.sigmoid(y))
        return out

    for t, (qt, kt) in enumerate(zip(conv(uq_ref, pq_ref, cwq_ref, cbq_ref),
                                     conv(uk_ref, pk_ref, cwk_ref, cbk_ref))):
        q_s[t] = qt
        k_s[t] = kt * HEAD_DIM ** -0.5

    ig = [g_ref[0, t, 0:1, :] + bif_ref[0, 0:1, :] for t in range(tokens)]
    lf = [_log_sigmoid(g_ref[0, t, 1:2, :] + bif_ref[0, 1:2, :]) for t in range(tokens)]
    b = [lf[0]]
    for t in range(1, tokens):
        b.append(b[-1] + lf[t])
    m0 = m_ref[0]
    nvec = n_ref[0]
    a = [bt + m0 for bt in b]
    dmat = [[b[t] - b[s] + ig[s] for s in range(t + 1)] for t in range(tokens)]
    m_t = [functools.reduce(jnp.maximum, dmat[t], a[t]) for t in range(tokens)]
    inter = [jnp.exp(a[t] - m_t[t]) for t in range(tokens)]
    smat = [[jnp.sum(q_s[t] * k_s[s], axis=0, keepdims=True) * jnp.exp(dmat[t][s] - m_t[t])
             for s in range(t + 1)] for t in range(tokens)]
    den = [inter[t] * jnp.sum(q_s[t] * nvec, axis=0, keepdims=True)
           + functools.reduce(lambda x, y: x + y, smat[t]) for t in range(tokens)]
    dnm = [jnp.maximum(jnp.abs(den[t]), jnp.exp(-m_t[t])) for t in range(tokens)]
    m_last = m_t[-1]
    w = [jnp.exp(b[-1] - b[s] + ig[s] - m_last) for s in range(tokens)]
    decay = jnp.exp(a[-1] - m_last)
    for s in range(tokens):
        vw_s[s] = v_ref[s] * w[s]

    def per_vdim(f, carry):
        row = pl.ds(f, 1)
        cf = c_ref[0, f]
        for t in range(tokens):
            qc_s[t, row, :] = jnp.sum(q_s[t] * cf, axis=0, keepdims=True)
        upd = decay * cf
        for s in range(tokens):
            upd = upd + vw_s[s, row, :] * k_s[s]
        cout_ref[0, f] = upd
        return carry
    lax.fori_loop(0, HEAD_DIM, per_vdim, 0)

    n_new = decay * nvec
    for s in range(tokens):
        n_new = n_new + w[s] * k_s[s]
    nout_ref[0] = n_new
    mout_ref[0] = m_last
    for t in range(tokens):
        num = inter[t] * qc_s[t]
        for s in range(t + 1):
            num = num + smat[t][s] * v_ref[s]
        hv = num / dnm[t] * jax.nn.sigmoid(o_ref[t])
        ms = jnp.mean(hv * hv, axis=0, keepdims=True)
        hm_ref[t] = hv * lax.rsqrt(ms + EPS) * mhg_ref[...]


def _mlstm_sample(u_t, g_t, cprev_t, cw_b, cb_b, bif_b, mhg_b, layer, c_all, c_new, n_t, m_t):
    tokens, _, nseq = u_t.shape
    e = HEAD_DIM
    args = [u_t, u_t, u_t, u_t, g_t, cprev_t, cprev_t, cw_b, cw_b, cb_b, cb_b, bif_b, mhg_b,
            c_all, n_t, m_t]
    extra_specs, aliases = [], {}
    if c_new is not None:
        aliases = {len(args): 1}
        extra_specs = [pl.BlockSpec(memory_space=pl.ANY)]
        args.append(c_new)
    tok_blk = lambda sec: pl.BlockSpec((tokens, e, nseq), lambda h, sec=sec: (0, sec * H_M + h, 0))
    prev_blk = lambda sec: pl.BlockSpec((CONV_W - 1, e, nseq), lambda h, sec=sec: (0, sec * H_M + h, 0))
    cw_blk = lambda sec: pl.BlockSpec((CONV_W, e, nseq), lambda h, sec=sec: (0, sec * H_M + h, 0))
    cb_blk = lambda sec: pl.BlockSpec((e, nseq), lambda h, sec=sec: (sec * H_M + h, 0))
    c_blk = pl.BlockSpec((None, 1, e, e, nseq), lambda h: (layer, h, 0, 0, 0))
    n_blk = pl.BlockSpec((1, e, nseq), lambda h: (h, 0, 0))
    m_blk = pl.BlockSpec((1, 1, nseq), lambda h: (h, 0, 0))
    kern = functools.partial(_mlstm_sample_kernel, tokens=tokens)
    return pl.pallas_call(
        kern,
        grid=(H_M,),
        in_specs=[tok_blk(0), tok_blk(1), tok_blk(2), tok_blk(3),
                  pl.BlockSpec((1, tokens, 2, nseq), lambda h: (h, 0, 0, 0)),
                  prev_blk(0), prev_blk(1), cw_blk(0), cw_blk(1), cb_blk(0), cb_blk(1),
                  pl.BlockSpec((1, 2, nseq), lambda h: (h, 0, 0)),
                  pl.BlockSpec((e, nseq), lambda h: (h, 0)),
                  c_blk, n_blk, m_blk] + extra_specs,
        out_specs=[pl.BlockSpec((tokens, e, nseq), lambda h: (0, h, 0)), c_blk, n_blk, m_blk],
        out_shape=[jax.ShapeDtypeStruct((tokens, W_M, nseq), F32),
                   jax.ShapeDtypeStruct(c_all.shape, F32),
                   jax.ShapeDtypeStruct(n_t.shape, F32),
                   jax.ShapeDtypeStruct(m_t.shape, F32)],
        input_output_aliases=aliases,
        scratch_shapes=[pltpu.VMEM((tokens, e, nseq), F32) for _ in range(4)],
        compiler_params=_params("parallel"),
        name="mlstm_sample",
    )(*args)


def _from_blockdiag(cbd):
    lead = cbd.shape[:-3]
    c0 = cbd[..., :HEAD_DIM, :HEAD_DIM]
    c1 = cbd[..., HEAD_DIM:, HEAD_DIM:]
    return jnp.stack([c0, c1], axis=-3).reshape(lead + (H_M, HEAD_DIM, HEAD_DIM))


ITEMS_PER_STAGE = (4, 3, 2, 1)


def _pipelined_loop(items, stages):
    depth = len(stages)
    group = next(g for g in ITEMS_PER_STAGE if items % g == 0)
    count = items // group

    def tick(t, static):
        for s in reversed(range(depth)):
            if not static or 0 <= t - s < count:
                for u in range(group):
                    stages[s]((t - s) * group + u)

    if count < depth:
        for t in range(count + depth - 1):
            tick(t, True)
        return
    for t in range(depth - 1):
        tick(t, True)

    def body(t, carry):
        tick(t, False)
        return carry
    lax.fori_loop(depth - 1, count, body, 0)
    for t in range(count, count + depth - 1):
        tick(t, True)

def _attn_prompt_kernel(q_ref, k_ref, v_ref, tab_ref, out_ref, obuf, lbuf, lg_s, p_s, s_s, l_s,
                        *, seq):
    col = lax.broadcasted_iota(jnp.int32, (CHUNK, LANE), 1)
    lane_lo = col < HEAD_DIM

    for bi, (win, dil) in enumerate(DIL_BRANCHES):
        assert win // dil == CHUNK
        nfirst = dil
        nrest = seq // CHUNK - dil
        assert 2 * (nfirst + nrest) <= lg_s.shape[0]

        def rows(st, dil=dil):
            return pl.ds(st, CHUNK) if dil == 1 else pl.ds(st, CHUNK, stride=dil)

        def place(i, with_prev, dil=dil, nfirst=nfirst):
            if not with_prev:
                return i, 2 * i, CHUNK
            start = i % dil + dil * CHUNK * (1 + i // dil)
            if dil == 1 and not isinstance(start, int):
                start = pl.multiple_of(start, CHUNK)
            return start, 2 * (nfirst + i), 2 * CHUNK

        def window(ref, start, with_prev, dil=dil, rows=rows):
            w = ref[rows(start), :]
            if with_prev:
                w = jnp.concatenate([ref[rows(start - dil * CHUNK), :], w], axis=0)
            return w.astype(BF16)

        def logits(i, with_prev, bi=bi, rows=rows, place=place, window=window):
            start, slot, width = place(i, with_prev)
            qb = q_ref[rows(start), :]
            kw = window(k_ref, start, with_prev)
            for hh in range(2):
                sel = lane_lo if hh == 0 else jnp.logical_not(lane_lo)
                qm = jnp.where(sel, qb, 0.0).astype(BF16)
                lg_s[slot + hh, :, 0:width] = (_dot_nt(qm, kw)
                                               + tab_ref[bi, hh, :, 2 * CHUNK - width:])

        def rowmax(i, with_prev, place=place):
            _, slot, width = place(i, with_prev)
            for j in (slot, slot + 1):
                mx = jnp.max(lg_s[j, :, 0:width], axis=1, keepdims=True)
                l_s[j] = jnp.broadcast_to(mx, (CHUNK, LANE))

        def softmax(i, with_prev, place=place):
            _, slot, width = place(i, with_prev)
            for j in (slot, slot + 1):
                mx = l_s[j]
                pr = jnp.concatenate([jnp.exp(lg_s[j, :, k:k + LANE] - mx)
                                      for k in range(0, width, LANE)], axis=1)
                sm = jnp.broadcast_to(jnp.sum(pr, axis=1, keepdims=True), (CHUNK, LANE))
                p_s[j, :, 0:width] = pr.astype(BF16)
                s_s[j] = sm
                l_s[j] = mx + jnp.log(sm)

        def weighted(i, with_prev, bi=bi, rows=rows, place=place, window=window):
            start, slot, width = place(i, with_prev)
            vw = window(v_ref, start, with_prev)
            pv0 = jnp.dot(p_s[slot, :, 0:width], vw, preferred_element_type=F32)
            pv1 = jnp.dot(p_s[slot + 1, :, 0:width], vw, preferred_element_type=F32)
            obuf[bi, rows(start), :] = jnp.where(lane_lo, pv0 / s_s[slot], pv1 / s_s[slot + 1])
            lbuf[bi, rows(start), :] = jnp.where(lane_lo, l_s[slot], l_s[slot + 1])

        for with_prev, count in ((False, nfirst), (True, nrest)):
            _pipelined_loop(count, [functools.partial(stage, with_prev=with_prev)
                                    for stage in (logits, rowmax, softmax, weighted)])

    step = 2 * CHUNK

    def combine(i, carry):
        r = pl.ds(pl.multiple_of(i * step, step), step)
        l0, l1, l2 = lbuf[0, r, :], lbuf[1, r, :], lbuf[2, r, :]
        mx = jnp.maximum(jnp.maximum(l0, l1), l2)
        e0, e1, e2 = jnp.exp(l0 - mx), jnp.exp(l1 - mx), jnp.exp(l2 - mx)
        den = e0 + e1 + e2
        out_ref[r, :] = ((e0 / den) * obuf[0, r, :] + (e1 / den) * obuf[1, r, :]
                         + (e2 / den) * obuf[2, r, :])
        return carry
    lax.fori_loop(0, seq // step, combine, 0)


def _attn_prompt(z, tab, *, nseq, seq):
    n = z.shape[0]
    npair = H_A // 2
    nb = len(DIL_BRANCHES)
    nslot = 2 * (seq // CHUNK)
    kern = functools.partial(_attn_prompt_kernel, seq=seq)
    return pl.pallas_call(
        kern,
        grid=(nseq, npair),
        in_specs=[pl.BlockSpec((seq, LANE), lambda b, p: (b, Z_AQ // LANE + p)),
                  pl.BlockSpec((seq, LANE), lambda b, p: (b, Z_AK // LANE + p)),
                  pl.BlockSpec((seq, LANE), lambda b, p: (b, Z_AV // LANE + p)),
                  pl.BlockSpec((nb, 2, CHUNK, 2 * CHUNK), lambda b, p: (0, p, 0, 0))],
        out_specs=pl.BlockSpec((seq, LANE), lambda b, p: (b, p)),
        out_shape=jax.ShapeDtypeStruct((n, W_A), F32),
        scratch_shapes=[pltpu.VMEM((nb, seq, LANE), F32),
                        pltpu.VMEM((nb, seq, LANE), F32),
                        pltpu.VMEM((nslot, CHUNK, 2 * CHUNK), F32),
                        pltpu.VMEM((nslot, CHUNK, 2 * CHUNK), BF16),
                        pltpu.VMEM((nslot, CHUNK, LANE), F32),
                        pltpu.VMEM((nslot, CHUNK, LANE), F32)],
        compiler_params=_params("parallel", "parallel"),
        name="attn_prompt",
    )(z, z, z, tab)


def _prompt_bias_table(rel_bias):
    qi = np.arange(CHUNK)[:, None]
    kj = np.arange(2 * CHUNK)[None, :]
    delta = qi + CHUNK - kj
    tabs = []
    for win, dil in DIL_BRANCHES:
        wc = win // dil
        valid = (delta >= 0) & (delta <= wc)
        bias = _bias_lookup(rel_bias, _rel_bucket(np.clip(delta, 0, None) * dil))
        tabs.append(jnp.where(valid[None], bias, NEG))
    return jnp.stack(tabs)


def _bias_lookup(rel_bias, bucket):
    onehot = (jnp.asarray(bucket)[..., None] == jnp.arange(N_BUCKETS)).astype(F32)
    return jnp.einsum("...n,nh->h...", onehot, rel_bias.astype(F32),
                      precision=lax.Precision.HIGHEST)


def _bdot(a, b, contract):
    return lax.dot_general(a, b, ((contract[0], contract[1]), ((0,), (0,))),
                           preferred_element_type=F32)


def _attn_sample_kernel(li_ref, q_ref, kn_ref, vn_ref, kt_ref, vt_ref, *rest, tokens, wbuf):
    del li_ref
    nb = len(DIL_BRANCHES)
    tab_refs, tabn_ref, out_ref = rest[:nb], rest[nb], rest[nb + 1]
    q = q_ref[0].astype(BF16)
    kt = kt_ref[0, 0].astype(BF16)
    vt = vt_ref[0, 0].astype(BF16)
    pad = jnp.zeros((H_A, LANE - tokens, HEAD_DIM), F32)
    kn = jnp.concatenate([kn_ref[0], pad], axis=1).astype(BF16)
    vn = jnp.concatenate([vn_ref[0], pad], axis=1).astype(BF16)
    lg_cache = _bdot(q, kt, ((2,), (1,)))
    lg_new = _bdot(q, kn, ((2,), (2,)))

    probs, probs_new, stats = [], [], []
    for bi, (win, dil) in enumerate(DIL_BRANCHES):
        lo = wbuf - win
        lg = lg_cache[:, :, lo:] + tab_refs[bi][...]
        ln = lg_new + tabn_ref[bi]
        mx = jnp.maximum(jnp.max(lg, axis=-1, keepdims=True), jnp.max(ln, axis=-1, keepdims=True))
        pr = jnp.exp(lg - mx)
        pn = jnp.exp(ln - mx)
        sm = jnp.sum(pr, axis=-1, keepdims=True) + jnp.sum(pn, axis=-1, keepdims=True)
        if lo > 0:
            pr = jnp.concatenate([jnp.zeros((H_A, tokens, lo), F32), pr], axis=-1)
        probs.append(pr)
        probs_new.append(pn)
        stats.append((mx, sm))
    p_all = jnp.concatenate(probs, axis=1).astype(BF16)
    pn_all = jnp.concatenate(probs_new, axis=1).astype(BF16)
    o_all = _bdot(p_all, vt, ((2,), (2,))) + _bdot(pn_all, vn, ((2,), (1,)))

    lses = [mx + jnp.log(sm) for mx, sm in stats]
    top = functools.reduce(jnp.maximum, lses)
    es = [jnp.exp(l - top) for l in lses]
    den = functools.reduce(lambda a, b: a + b, es)
    out = None
    for bi in range(nb):
        term = (es[bi] / den) * (o_all[:, bi * tokens:(bi + 1) * tokens] / stats[bi][1])
        out = term if out is None else out + term
    out_ref[0] = out


def _attn_sample(layer, q, kn, vn, cache_kt, cache_vt, tabs, tabn):
    nseq, _, tokens, _ = q.shape
    wbuf = cache_kt.shape[-1]
    new_spec = pl.BlockSpec((1, H_A, tokens, HEAD_DIM), lambda b, li: (b, 0, 0, 0))
    cache_spec = pl.BlockSpec((1, 1, H_A, HEAD_DIM, wbuf), lambda b, li: (li[0], b, 0, 0, 0))
    tab_specs = [pl.BlockSpec(t.shape, lambda b, li: (0, 0, 0)) for t in tabs]
    kern = functools.partial(_attn_sample_kernel, tokens=tokens, wbuf=wbuf)
    return pl.pallas_call(
        kern,
        grid_spec=pltpu.PrefetchScalarGridSpec(
            num_scalar_prefetch=1,
            grid=(nseq,),
            in_specs=[new_spec, new_spec, new_spec, cache_spec, cache_spec, *tab_specs,
                      pl.BlockSpec(tabn.shape, lambda b, li: (0, 0, 0, 0))],
            out_specs=new_spec),
        out_shape=jax.ShapeDtypeStruct((nseq, H_A, tokens, HEAD_DIM), F32),
        compiler_params=_params("parallel"),
        name="attn_sample",
    )(layer, q, kn, vn, cache_kt, cache_vt, *tabs, tabn)


def _sample_bias_tables(rel_bias, tokens, wbuf):
    assert wbuf >= max(w for w, _ in DIL_BRANCHES)
    tok = np.arange(tokens)[:, None]
    tabs = []
    for win, dil in DIL_BRANCHES:
        pos = np.arange(wbuf - win, wbuf)[None, :]
        dist = wbuf + tok - pos
        valid = (dist % dil == 0) & (dist <= win)
        tabs.append(jnp.where(valid[None], _bias_lookup(rel_bias, _rel_bucket(dist)), NEG))
    other = np.arange(LANE)[None, :]
    dist = tok - other
    news = []
    for win, dil in DIL_BRANCHES:
        valid = (dist >= 0) & (dist % dil == 0) & (dist <= win)
        news.append(jnp.where(valid[None], _bias_lookup(rel_bias, _rel_bucket(np.clip(dist, 0, None))),
                              NEG))
    return tabs, jnp.stack(news)


def _post_kernel(x_ref, hm_ref, ha_ref, pe_ref, wo_ref, g2_ref, wg_ref, wu_ref, wd_ref, g3_ref,
                 wpg_ref, wpe_ref, out_ref, x1_s, h2_s, acc, *, nff):
    j = pl.program_id(1)

    @pl.when(j == 0)
    def _mix():
        x1 = (x_ref[...]
              + jnp.dot(hm_ref[...].astype(BF16), wo_ref[0:W_M, :], preferred_element_type=F32)
              + jnp.dot(ha_ref[...].astype(BF16), wo_ref[W_M:, :], preferred_element_type=F32))
        x1_s[...] = x1
        ms = jnp.mean(x1 * x1, axis=-1, keepdims=True)
        h2_s[...] = (x1 * lax.rsqrt(ms + EPS) * g2_ref[...]).astype(BF16)
        acc[...] = jnp.zeros(acc.shape, F32)

    h2 = h2_s[...]
    gate = jnp.dot(h2, wg_ref[...], preferred_element_type=F32)
    up = jnp.dot(h2, wu_ref[...], preferred_element_type=F32)
    act = (gate * jax.nn.sigmoid(gate) * up).astype(BF16)
    acc[...] += jnp.dot(act, wd_ref[...], preferred_element_type=F32)

    @pl.when(j == nff - 1)
    def _fin():
        x2 = x1_s[...] + acc[...]
        ms = jnp.mean(x2 * x2, axis=-1, keepdims=True)
        h3 = (x2 * lax.rsqrt(ms + EPS) * g3_ref[...]).astype(BF16)
        pg = jax.nn.sigmoid(jnp.dot(h3, wpg_ref[...], preferred_element_type=F32))
        pp = jnp.dot(pe_ref[...].astype(BF16), wpe_ref[...], preferred_element_type=F32)
        out_ref[...] = x2 + pg * pp


def _post(x, hm, ha, pe, wo, g2, wgu, wd, g3, wpg, wpe):
    n = x.shape[0]
    tm = min(ROWS_POST, n)
    nff = D_FF // FF_BLOCK
    row = lambda i, j: (i, 0)
    fixed = lambda shape: pl.BlockSpec(shape, lambda i, j: (0, 0), pipeline_mode=pl.Buffered(1))
    kern = functools.partial(_post_kernel, nff=nff)
    return pl.pallas_call(
        kern,
        grid=(n // tm, nff),
        in_specs=[pl.BlockSpec((tm, D_MODEL), row),
                  pl.BlockSpec((tm, W_M), row),
                  pl.BlockSpec((tm, W_A), row),
                  pl.BlockSpec((tm, D_PLE), row),
                  fixed((D_MODEL, D_MODEL)),
                  fixed((1, D_MODEL)),
                  pl.BlockSpec((D_MODEL, FF_BLOCK), lambda i, j: (0, j)),
                  pl.BlockSpec((D_MODEL, FF_BLOCK), lambda i, j: (0, nff + j)),
                  pl.BlockSpec((FF_BLOCK, D_MODEL), lambda i, j: (j, 0)),
                  fixed((1, D_MODEL)),
                  fixed((D_MODEL, D_MODEL)),
                  fixed((D_PLE, D_MODEL))],
        out_specs=pl.BlockSpec((tm, D_MODEL), row),
        out_shape=jax.ShapeDtypeStruct((n, D_MODEL), F32),
        scratch_shapes=[pltpu.VMEM((tm, D_MODEL), F32),
                        pltpu.VMEM((tm, D_MODEL), BF16),
                        pltpu.VMEM((tm, D_MODEL), F32)],
        compiler_params=_params("parallel", "arbitrary"),
        name="post",
    )(x, hm, ha, pe, wo, g2, wgu, wgu, wd, g3, wpg, wpe)


def _mixer_tail(x, hm, ha, pe, lw):
    return _post(x, hm, ha, pe, lw["w_out"], lw["norm2_g"], lw["w_gu"], lw["w_down"],
                 lw["ple_norm_g"], lw["w_pg"], lw["w_pe"])


def _layer(xp, xs, layer, filled, lw, *, depth, batch, seq, dec_batch, dec_seq, cache_k, cache_v,
           state_c, tab_p, tab_s, tab_n, seg):
    def proj(x, win=None):
        return _in_proj(x, lw["norm1_g"], lw["w_in"], lw["q_norm_g"], lw["k_norm_g"], seg, win)

    kt, vt, c_new = filled
    zp, kt, vt = proj(xp, (layer, depth, batch, seq, kt, vt))
    hm, pc, pn, pm = _mlstm_prompt(zp, lw["conv_w"], lw["conv_b"], lw["b_i"], lw["b_f"],
                                   lw["mh_norm_g"], nseq=batch, nchunks=seq // CHUNK)
    ha = _attn_prompt(zp, tab_p, nseq=batch, seq=seq)
    xp_new = _mixer_tail(xp, hm, ha, lw["p_prompt"], lw)
    zp3 = zp.reshape(batch, seq, Z_W)
    p_out = (_from_blockdiag(pc),
             pn.reshape(batch, H_M, HEAD_DIM),
             pm[:, 0, :H_M],
             zp3[:, seq - (CONV_W - 1):, Z_QK:Z_QK + 2 * W_M])

    zs, = proj(xs)
    zs3 = zs.reshape(dec_batch, dec_seq, Z_W)
    lanes = lambda a: jnp.broadcast_to(a[..., None], a.shape + (dec_batch,))
    gates = jnp.stack([zs3[:, :, Z_GI:Z_GI + H_M], zs3[:, :, Z_GF:Z_GF + H_M]], axis=2)
    hm_t, c_new, sn, sm = _mlstm_sample(
        jnp.transpose(zs3[:, :, :Z_AQ], (1, 2, 0)),
        jnp.transpose(gates, (3, 1, 2, 0)),
        jnp.transpose(lw["state_conv"], (1, 2, 0)),
        lanes(lw["conv_w"]), lanes(lw["conv_b"][0]),
        lanes(jnp.stack([lw["b_i"][0, :H_M], lw["b_f"][0, :H_M]], axis=1)), lanes(lw["mh_norm_g"][0]),
        layer, state_c, c_new,
        jnp.transpose(lw["state_n"], (1, 2, 0)),
        jnp.transpose(lw["state_m"], (1, 0))[:, None, :])
    hm = jnp.transpose(hm_t, (2, 0, 1)).reshape(dec_batch * dec_seq, W_M)
    heads = (dec_batch, dec_seq, H_A, HEAD_DIM)
    q_s = zs3[:, :, Z_AQ:Z_AQ + W_A].reshape(heads)
    k_s = zs3[:, :, Z_AK:Z_AK + W_A].reshape(heads)
    v_s = zs3[:, :, Z_AV:Z_AV + W_A].reshape(heads)
    by_head = lambda t: jnp.transpose(t, (0, 2, 1, 3))
    ha = _attn_sample(lw["layer"], by_head(q_s), by_head(k_s), by_head(v_s), cache_k, cache_v,
                      tab_s, tab_n)
    ha = by_head(ha).reshape(dec_batch * dec_seq, W_A)
    xs_new = _mixer_tail(xs, hm, ha, lw["p_sample"], lw)
    conv_rows = jnp.concatenate([lw["state_conv"], zs3[:, :, Z_QK:Z_QK + 2 * W_M]], axis=1)
    s_out = (k_s, v_s,
             jnp.transpose(sn, (2, 0, 1)),
             jnp.transpose(sm[:, 0, :], (1, 0)),
             conv_rows[:, -(CONV_W - 1):])
    return xp_new, xs_new, (kt, vt, c_new), p_out, s_out


def kernel(x_prompt, x_sample, p_prompt, p_sample, cache_attn_k, cache_attn_v, state_mlstm_C, state_mlstm_n, state_mlstm_m, state_conv, rel_bias, norm1_g, w_in, b_if, conv_w, conv_b, mh_norm_g, q_norm_g, k_norm_g, w_out, norm2_g, w_gu, w_down, ple_norm_g, w_pe, w_pg):
    batch, seq, _ = x_prompt.shape
    dec_batch, dec_seq, _ = x_sample.shape
    depth = w_in.shape[0]
    assert seq % (CHUNK * max(d for _, d in DIL_BRANCHES)) == 0
    assert seq <= max(w for w, _ in DIL_BRANCHES)

    n_m = 4 * W_M
    gate_pad = jnp.zeros((depth, D_MODEL, LANE - H_M), w_in.dtype)
    w_in_z = jnp.concatenate(
        [w_in[:, :, :n_m], w_in[:, :, n_m + 2 * H_M:],
         w_in[:, :, n_m:n_m + H_M], gate_pad, w_in[:, :, n_m + H_M:n_m + 2 * H_M], gate_pad],
        axis=-1).astype(BF16)
    b_gate = lambda lo: jnp.pad(b_if[:, lo:lo + H_M], ((0, 0), (0, LANE - H_M)))[:, None, :]

    head_of_lane = jnp.arange(W_A) // HEAD_DIM
    seg = (head_of_lane[:, None] == head_of_lane[None, :]).astype(BF16)
    tab_p = _prompt_bias_table(rel_bias)
    tab_s, tab_n = _sample_bias_tables(rel_bias, dec_seq, cache_attn_k.shape[2])
    cache_kt = jnp.transpose(cache_attn_k, (0, 1, 3, 4, 2))
    cache_vt = jnp.transpose(cache_attn_v, (0, 1, 3, 4, 2))

    row = lambda a: a[:, None, :]
    per_layer = dict(
        layer=jnp.arange(depth, dtype=jnp.int32)[:, None],
        norm1_g=row(norm1_g), w_in=w_in_z,
        q_norm_g=row(jnp.tile(q_norm_g, (1, H_A))), k_norm_g=row(jnp.tile(k_norm_g, (1, H_A))),
        conv_w=conv_w, conv_b=row(conv_b),
        b_i=b_gate(0), b_f=b_gate(H_M),
        mh_norm_g=row(mh_norm_g),
        w_out=w_out.astype(BF16), norm2_g=row(norm2_g),
        w_gu=w_gu.astype(BF16), w_down=w_down.astype(BF16),
        ple_norm_g=row(ple_norm_g), w_pg=w_pg.astype(BF16), w_pe=w_pe.astype(BF16),
        p_prompt=p_prompt.reshape(depth, batch * seq, D_PLE),
        p_sample=p_sample.reshape(depth, dec_batch * dec_seq, D_PLE),
        state_conv=state_conv, state_n=state_mlstm_n, state_m=state_mlstm_m,
    )
    xp = x_prompt.reshape(batch * seq, D_MODEL)
    xs = x_sample.reshape(dec_batch * dec_seq, D_MODEL)
    p_outs, s_outs = [], []
    filled = (None, None, None)
    state_c = jnp.transpose(state_mlstm_C, (0, 2, 3, 4, 1))
    for i in range(depth):
        lw = {name: val[i] for name, val in per_layer.items()}
        xp, xs, filled, p_out, s_out = _layer(
            xp, xs, i, filled, lw, depth=depth, batch=batch, seq=seq, dec_batch=dec_batch,
            dec_seq=dec_seq, cache_k=cache_kt, cache_v=cache_vt, state_c=state_c,
            tab_p=tab_p, tab_s=tab_s, tab_n=tab_n, seg=seg)
        p_outs.append(p_out)
        s_outs.append(s_out)
    stack = lambda outs: tuple(jnp.stack(leaf) for leaf in zip(*outs))
    kt, vt, c_new = filled
    window = lambda t: jnp.transpose(t.reshape(depth, batch, H_A, HEAD_DIM, seq), (0, 1, 4, 2, 3))
    s_k, s_v, *s_rest = stack(s_outs)
    return (xp.reshape(batch, seq, D_MODEL), xs.reshape(dec_batch, dec_seq, D_MODEL),
            window(kt), window(vt), *stack(p_outs),
            s_k, s_v, jnp.transpose(c_new, (0, 4, 1, 2, 3)), *s_rest)
```
